```python
import jax, jax.numpy as jnp
from jax import lax
import numpy as np

D_MODEL = 1024
BATCH = 4
SEQ = 8192
DEPTH = 1

EPS = 1e-6
NEG_INF = -1e30
SGU_CHUNK = 128
SGU_GROUPS = 8
SGU_GROUP_DIM = 128
SGU_WIDTH = SGU_GROUPS * SGU_GROUP_DIM
ATT_HEADS = 8
ATT_HEAD_DIM = 64
ATT_PATTERNS = ((128, 1), (512, 4), (2048, 16))
ATT_GROUPS = len(ATT_PATTERNS)
ATT_Q_WIDTH = ATT_GROUPS * ATT_HEADS * ATT_HEAD_DIM
ATT_KV_WIDTH = ATT_HEADS * ATT_HEAD_DIM
N_BRANCHES = 2
IN_SPLITS = (SGU_WIDTH, SGU_WIDTH, ATT_Q_WIDTH, ATT_KV_WIDTH, ATT_KV_WIDTH, D_MODEL, D_MODEL)
IN_WIDTH = sum(IN_SPLITS)
PEER_HEADS = 8
PEER_N_KEYS = 128
PEER_N_EXPERTS = PEER_N_KEYS * PEER_N_KEYS
PEER_KEY_DIM = 256
PEER_HALF = PEER_KEY_DIM // 2
PEER_TOPK = 16
PEER_TOKEN_BLOCK = 128

kernel_name = 'hybrid_sgu_dilated_attn_peer_block'


def rmsnorm(x, g):
    xf = x.astype(jnp.float32)
    y = xf * lax.rsqrt(jnp.mean(xf * xf, axis=-1, keepdims=True) + EPS) * g.astype(jnp.float32)
    return y.astype(x.dtype)


def alibi_slopes(n_heads):
    return 2.0 ** (-8.0 * jnp.arange(1, n_heads + 1, dtype=jnp.float32) / n_heads)


def spatial_gating_mixer(z, norm_g, w_s, b_s):
    B, S, _ = z.shape
    u, v = jnp.split(z, 2, axis=-1)
    v = rmsnorm(v, norm_g).reshape(B, S // SGU_CHUNK, SGU_CHUNK, SGU_GROUPS, SGU_GROUP_DIM)
    v = jnp.einsum('gts,bnsgc->bntgc', w_s, v) + b_s.T[None, None, :, :, None]
    return u * v.reshape(B, S, SGU_WIDTH)


def dilated_band_attention(q, k, v, dilation, n_side, slopes):
    B, S, H, E = q.shape
    span = dilation * n_side
    Sp = -(-S // span) * span
    pad = [(0, 0), (0, Sp - S), (0, 0), (0, 0)]
    q, k, v = (jnp.pad(t.astype(jnp.float32), pad) for t in (q, k, v))
    M = Sp // dilation
    nb = M // n_side
    qb = q.reshape(B, nb, n_side, dilation, H, E)

    def windows(t):
        t = t.reshape(B, M, dilation, H, E)
        t = jnp.pad(t, [(0, 0), (n_side, n_side), (0, 0), (0, 0), (0, 0)])
        t = t.reshape(B, nb + 2, n_side, dilation, H, E)
        return jnp.concatenate([t[:, :-2], t[:, 1:-1], t[:, 2:]], axis=2)

    kw, vw = windows(k), windows(v)
    rel = jnp.arange(3 * n_side)[None, :] - n_side - jnp.arange(n_side)[:, None]
    band = jnp.abs(rel) <= n_side
    m_k = jnp.arange(nb)[:, None, None] * n_side + jnp.arange(3 * n_side)[None, None, :] - n_side
    pos_k = m_k * dilation + jnp.arange(dilation)[None, :, None]
    key_ok = (m_k >= 0) & (pos_k < S)
    mask = band[None, None] & key_ok[:, :, None, :]
    bias = -slopes[:, None, None] * (jnp.abs(rel) * dilation).astype(jnp.float32)[None]
    s = jnp.einsum('bnidhe,bnjdhe->bndhij', qb, kw) * (E ** -0.5) + bias
    s = jnp.where(mask[None, :, :, None], s, NEG_INF)
    lse = jax.nn.logsumexp(s, axis=-1)
    p = jnp.exp(s - lse[..., None])
    o = jnp.einsum('bndhij,bnjdhe->bnidhe', p, vw).reshape(B, Sp, H, E)[:, :S]
    lse = lse.transpose(0, 1, 4, 2, 3).reshape(B, Sp, H)[:, :S]
    return o, lse


def dilated_attention_mixer(q_all, k, v):
    B, S = q_all.shape[:2]
    slopes = alibi_slopes(ATT_HEADS)
    outs, lses = [], []
    for g, (window, dilation) in enumerate(ATT_PATTERNS):
        o, l = dilated_band_attention(q_all[:, :, g], k, v, dilation, window // (2 * dilation), slopes)
        outs.append(o)
        lses.append(l)
    wts = jax.nn.softmax(jnp.stack(lses, axis=0), axis=0)
    o = jnp.einsum('gbsh,gbshe->bshe', wts, jnp.stack(outs, axis=0))
    return o.reshape(B, S, ATT_KV_WIDTH)


def peer_ffn(h, w_q, sub_keys, u_tab, v_tab):
    B, S, D = h.shape
    q = (h @ w_q).reshape(B, S, PEER_HEADS, 2, PEER_HALF).astype(jnp.float32)
    s = jnp.einsum('bshpk,hpnk->bshpn', q, sub_keys.astype(jnp.float32))
    top_s, top_i = lax.top_k(s, PEER_TOPK)
    cand_s = top_s[..., 0, :, None] + top_s[..., 1, None, :]
    cand_i = top_i[..., 0, :, None] * PEER_N_KEYS + top_i[..., 1, None, :]
    best_s, best_j = lax.top_k(cand_s.reshape(B, S, PEER_HEADS, PEER_TOPK * PEER_TOPK), PEER_TOPK)
    expert = jnp.take_along_axis(cand_i.reshape(B, S, PEER_HEADS, PEER_TOPK * PEER_TOPK), best_j, axis=-1)
    gate = jax.nn.softmax(best_s, axis=-1)
    n_blocks = (B * S) // PEER_TOKEN_BLOCK
    xs = (h.reshape(n_blocks, PEER_TOKEN_BLOCK, D),
          expert.reshape(n_blocks, PEER_TOKEN_BLOCK, PEER_HEADS * PEER_TOPK),
          gate.reshape(n_blocks, PEER_TOKEN_BLOCK, PEER_HEADS * PEER_TOPK))

    def block(args):
        hb, eb, gb = args
        a = jnp.einsum('ckd,cd->ck', u_tab[eb], hb)
        act = jax.nn.gelu(a.astype(jnp.float32), approximate=False) * gb
        return jnp.einsum('ck,ckd->cd', act.astype(hb.dtype), v_tab[eb])

    return lax.map(block, xs).reshape(B, S, D)


def setup_inputs(seed: int = 0) -> dict:
    key = jax.random.key(seed)
    ks = jax.random.split(key, 16)
    f32 = jnp.float32
    L, D = DEPTH, D_MODEL
    nrm = lambda k, shape, scale: jax.random.normal(k, shape, f32) * scale
    return {
        'x': jax.random.normal(ks[0], (BATCH, SEQ, D), f32),
        'norm_mix_g': 1.0 + nrm(ks[1], (L, D), 0.02),
        'w_in': nrm(ks[2], (L, D, IN_WIDTH), D ** -0.5),
        'sgu_norm_g': 1.0 + nrm(ks[3], (L, SGU_WIDTH), 0.02),
        'sgu_w': nrm(ks[4], (L, SGU_GROUPS, SGU_CHUNK, SGU_CHUNK), SGU_CHUNK ** -0.5),
        'sgu_b': 1.0 + nrm(ks[5], (L, SGU_GROUPS, SGU_CHUNK), 0.02),
        'w_branch_a': nrm(ks[6], (L, SGU_WIDTH, D), SGU_WIDTH ** -0.5),
        'w_branch_b': nrm(ks[7], (L, ATT_KV_WIDTH, D), ATT_KV_WIDTH ** -0.5),
        'w_out': nrm(ks[8], (L, D, D), D ** -0.5),
        'norm_ffn_g': 1.0 + nrm(ks[9], (L, D), 0.02),
        'peer_wq': nrm(ks[10], (L, D, PEER_HEADS * PEER_KEY_DIM), D ** -0.5),
        'peer_subkeys': nrm(ks[11], (L, PEER_HEADS, 2, PEER_N_KEYS, PEER_HALF), PEER_HALF ** -0.5),
        'peer_u': nrm(ks[12], (L, PEER_N_EXPERTS, D), D ** -0.5),
        'peer_v': nrm(ks[13], (L, PEER_N_EXPERTS, D), PEER_HEADS ** -0.5),
        'norm_final_g': 1.0 + nrm(ks[14], (D,), 0.02),
    }


def reference(x, norm_mix_g, w_in, sgu_norm_g, sgu_w, sgu_b, w_branch_a, w_branch_b, w_out,
              norm_ffn_g, peer_wq, peer_subkeys, peer_u, peer_v, norm_final_g):
    B, S, _ = x.shape
    split_at = np.cumsum(IN_SPLITS)[:-1].tolist()
    for l in range(DEPTH):
        h = rmsnorm(x, norm_mix_g[l])
        proj = h @ w_in[l]
        za_u, za_v, q_all, k, v, gate_a, gate_b = jnp.split(proj, split_at, axis=-1)
        z_a = jax.nn.gelu(jnp.concatenate([za_u, za_v], axis=-1), approximate=False)
        y_a = spatial_gating_mixer(z_a, sgu_norm_g[l], sgu_w[l], sgu_b[l])
        y_b = dilated_attention_mixer(
            q_all.reshape(B, S, ATT_GROUPS, ATT_HEADS, ATT_HEAD_DIM),
            k.reshape(B, S, ATT_HEADS, ATT_HEAD_DIM),
            v.reshape(B, S, ATT_HEADS, ATT_HEAD_DIM)).astype(x.dtype)
        merged = (jax.nn.sigmoid(gate_a) * (y_a @ w_branch_a[l])
                  + jax.nn.sigmoid(gate_b) * (y_b @ w_branch_b[l]))
        x = x + merged @ w_out[l]
        x = x + peer_ffn(rmsnorm(x, norm_ffn_g[l]), peer_wq[l], peer_subkeys[l], peer_u[l], peer_v[l])
    return rmsnorm(x, norm_final_g)
```

```python
import functools

import numpy as np
import jax
import jax.numpy as jnp
from jax import lax
from jax.experimental import pallas as pl
from jax.experimental.pallas import tpu as pltpu

F32 = jnp.float32
BF16 = jnp.bfloat16
I32 = jnp.int32

D_MODEL = 1024
EPS = 1e-6
NEG_INF = -1e30
SGU_CHUNK = 128
SGU_GROUPS = 8
SGU_WIDTH = 1024
ATT_HEADS = 8
ATT_HEAD_DIM = 64
ATT_PATTERNS = ((128, 1), (512, 4), (2048, 16))
ATT_GROUPS = 3
ATT_KV_WIDTH = ATT_HEADS * ATT_HEAD_DIM
ATT_Q_WIDTH = ATT_GROUPS * ATT_KV_WIDTH
PEER_HEADS = 8
PEER_N_KEYS = 128
PEER_N_EXPERTS = PEER_N_KEYS * PEER_N_KEYS
PEER_HALF = 128
PEER_TOPK = 16
PEER_PICKS = PEER_HEADS * PEER_TOPK

LANES = 128
SUBLANES = 8
ROW_WORDS = D_MODEL // 2
ROW_SUBLANES = ROW_WORDS // LANES
VMEM_LIMIT = 56 * 1024 * 1024

IN_CHUNK = 512
_IN_LAYOUT = (
    (0, 0, "gelu"), (0, 1, "gelu"), (1, 0, "gelu"), (1, 1, "gelu"),
    (2, 0, None), (2, 1, None), (2, 2, None), (3, 0, None), (4, 0, None),
    (5, 0, "sigmoid"), (5, 1, "sigmoid"), (6, 0, "sigmoid"), (6, 1, "sigmoid"),
)
_IN_WIDTHS = (1024, 1024, 1536, 512, 512, 1024, 1024)


def _params(*sem):
    return pltpu.CompilerParams(dimension_semantics=sem, vmem_limit_bytes=VMEM_LIMIT)


def _gelu(y):
    return 0.5 * y * (1.0 + lax.erf(y * np.float32(1.0 / np.sqrt(2.0))))


def _rms(x, g):
    return x * lax.rsqrt(jnp.mean(x * x, axis=-1, keepdims=True) + EPS) * g


def _full(shape):
    return pl.BlockSpec(shape, lambda *_: (0,) * len(shape))


def _in_proj_kernel(x_ref, g_ref, w_ref, *out_refs):
    h = _rms(x_ref[...], g_ref[...]).astype(BF16)
    for c, (slot, sub, epi) in enumerate(_IN_LAYOUT):
        y = jnp.dot(h, w_ref[:, c * IN_CHUNK:(c + 1) * IN_CHUNK], preferred_element_type=F32)
        if epi == "gelu":
            y = _gelu(y)
        elif epi == "sigmoid":
            y = jax.nn.sigmoid(y)
        out_refs[slot][:, sub * IN_CHUNK:(sub + 1) * IN_CHUNK] = y.astype(BF16)


def _in_proj(x2d, g, w_bf16, tm=512):
    n = x2d.shape[0]
    width = w_bf16.shape[1]
    return pl.pallas_call(
        _in_proj_kernel,
        grid=(n // tm,),
        in_specs=[pl.BlockSpec((tm, D_MODEL), lambda i: (i, 0)),
                  _full((1, D_MODEL)),
                  pl.BlockSpec((D_MODEL, width), lambda i: (0, 0), pipeline_mode=pl.Buffered(1))],
        out_specs=[pl.BlockSpec((tm, w), lambda i: (i, 0)) for w in _IN_WIDTHS],
        out_shape=[jax.ShapeDtypeStruct((n, w), BF16) for w in _IN_WIDTHS],
        compiler_params=_params("parallel"),
        name="in_proj",
    )(x2d, g, w_bf16)


def _sgu_kernel(u_ref, v_ref, ga_ref, ng_ref, ws_ref, bs_ref, wa_ref, out_ref):
    tm = u_ref.shape[0]
    vn = _rms(v_ref[...].astype(F32), ng_ref[...]).astype(BF16)
    chunks = []
    for c in range(tm // SGU_CHUNK):
        cols = []
        for g in range(SGU_GROUPS):
            blk = vn[c * SGU_CHUNK:(c + 1) * SGU_CHUNK, g * LANES:(g + 1) * LANES]
            cols.append(jnp.dot(ws_ref[g], blk, preferred_element_type=F32) + bs_ref[g])
        chunks.append(jnp.concatenate(cols, axis=1))
    mixed = jnp.concatenate(chunks, axis=0)
    ya = (u_ref[...].astype(F32) * mixed).astype(BF16)
    pa = jnp.dot(ya, wa_ref[...], preferred_element_type=F32) * ga_ref[...].astype(F32)
    out_ref[...] = pa.astype(BF16)


def _sgu(u, v, ga, ng, ws_bf16, bs_b, wa_bf16, tm=512):
    n = u.shape[0]
    row = pl.BlockSpec((tm, SGU_WIDTH), lambda i: (i, 0))
    return pl.pallas_call(
        _sgu_kernel,
        grid=(n // tm,),
        in_specs=[row, row, row, _full((1, SGU_WIDTH)),
                  _full((SGU_GROUPS, SGU_CHUNK, SGU_CHUNK)),
                  _full((SGU_GROUPS, SGU_CHUNK, LANES)),
                  _full((SGU_WIDTH, D_MODEL))],
        out_specs=pl.BlockSpec((tm, D_MODEL), lambda i: (i, 0)),
        out_shape=jax.ShapeDtypeStruct((n, D_MODEL), BF16),
        compiler_params=_params("parallel"),
        name="sgu",
    )(u, v, ga, ng, ws_bf16, bs_b, wa_bf16)


def _attn_kernel(q_ref, kp_ref, kc_ref, kn_ref, vp_ref, vc_ref, vn_ref, o_ref, lse_ref,
                 *, dilation, n_side, sub_len):
    tq = q_ref.shape[0]
    win = tq + 2 * n_side
    i = pl.program_id(2)
    k = jnp.concatenate([kp_ref[...], kc_ref[...], kn_ref[...]], axis=0)
    v = jnp.concatenate([vp_ref[...], vc_ref[...], vn_ref[...]], axis=0)
    row = lax.broadcasted_iota(I32, (tq, win), 0)
    col = lax.broadcasted_iota(I32, (tq, win), 1)
    rel = col - n_side - row
    pos = i * tq - n_side + col
    valid = (jnp.abs(rel) <= n_side) & (pos >= 0) & (pos < sub_len)
    negdist = -(jnp.abs(rel) * dilation).astype(F32)
    lane = lax.broadcasted_iota(I32, (tq, LANES), 1)
    low_half = lane < ATT_HEAD_DIM
    scale = np.float32(ATT_HEAD_DIM ** -0.5)
    for pair in range(ATT_HEADS // 2):
        sl = slice(pair * LANES, (pair + 1) * LANES)
        qp, kpair, vpair = q_ref[:, sl], k[:, sl], v[:, sl]
        outs, lses = [], []
        for j in range(2):
            slope = np.float32(2.0 ** (-8.0 * (pair * 2 + j + 1) / ATT_HEADS))
            qm = jnp.where(low_half if j == 0 else jnp.logical_not(low_half), qp, jnp.zeros_like(qp))
            s = lax.dot_general(qm, kpair, (((1,), (1,)), ((), ())), preferred_element_type=F32)
            s = jnp.where(valid, s * scale + slope * negdist, NEG_INF)
            m = jnp.max(s, axis=-1, keepdims=True)
            p = jnp.exp(s - m)
            l = jnp.sum(p, axis=-1, keepdims=True)
            o = jnp.dot(p.astype(BF16), vpair, preferred_element_type=F32)
            outs.append(o / l)
            lses.append(jnp.broadcast_to(m + jnp.log(l), (tq, LANES)))
        o_ref[:, sl] = jnp.where(low_half, outs[0], outs[1]).astype(o_ref.dtype)
        lse_ref[:, sl] = jnp.where(low_half, lses[0], lses[1])


def _attention_pattern(q3, k3, v3, group, dilation, n_side, tq=128):
    b, s, _ = q3.shape
    sub_len = s // dilation
    kvw = ATT_KV_WIDTH
    qv = q3.reshape(b, sub_len, dilation * ATT_Q_WIDTH)
    kv = k3.reshape(b, sub_len, dilation * kvw)
    vv = v3.reshape(b, sub_len, dilation * kvw)
    per = tq // n_side
    last = sub_len // n_side - 1
    cur = pl.BlockSpec((None, tq, kvw), lambda bi, r, i: (bi, i, r))
    prev = pl.BlockSpec((None, n_side, kvw), lambda bi, r, i: (bi, jnp.maximum(i * per - 1, 0), r))
    nxt = pl.BlockSpec((None, n_side, kvw), lambda bi, r, i: (bi, jnp.minimum((i + 1) * per, last), r))
    o, lse = pl.pallas_call(
        functools.partial(_attn_kernel, dilation=dilation, n_side=n_side, sub_len=sub_len),
        grid=(b, dilation, sub_len // tq),
        in_specs=[pl.BlockSpec((None, tq, kvw), lambda bi, r, i: (bi, i, r * ATT_GROUPS + group)),
                  prev, cur, nxt, prev, cur, nxt],
        out_specs=[cur, cur],
        out_shape=[jax.ShapeDtypeStruct((b, sub_len, dilation * kvw), BF16),
                   jax.ShapeDtypeStruct((b, sub_len, dilation * kvw), F32)],
        compiler_params=_params("parallel", "parallel", "parallel"),
        name=f"attn_d{dilation}",
    )(qv, kv, kv, kv, vv, vv, vv)
    return o.reshape(b * s, kvw), lse.reshape(b * s, kvw)


def _merge_kernel(x_ref, pa_ref, gb_ref, o1_ref, o2_ref, o3_ref, l1_ref, l2_ref, l3_ref,
                  wb_ref, wo_ref, g_ref, x1_ref, h2_ref):
    l1, l2, l3 = l1_ref[...], l2_ref[...], l3_ref[...]
    m = jnp.maximum(jnp.maximum(l1, l2), l3)
    w1, w2, w3 = jnp.exp(l1 - m), jnp.exp(l2 - m), jnp.exp(l3 - m)
    yb = (w1 * o1_ref[...].astype(F32) + w2 * o2_ref[...].astype(F32)
          + w3 * o3_ref[...].astype(F32)) / (w1 + w2 + w3)
    pb = jnp.dot(yb.astype(BF16), wb_ref[...], preferred_element_type=F32) * gb_ref[...].astype(F32)
    merged = (pa_ref[...].astype(F32) + pb).astype(BF16)
    x1 = x_ref[...] + jnp.dot(merged, wo_ref[...], preferred_element_type=F32)
    x1_ref[...] = x1
    h2_ref[...] = _rms(x1, g_ref[...])


def _merge(x2d, pa, gb, os_, ls_, wb_bf16, wo_bf16, g, tm=512):
    n = x2d.shape[0]
    wide = pl.BlockSpec((tm, D_MODEL), lambda i: (i, 0))
    half = pl.BlockSpec((tm, ATT_KV_WIDTH), lambda i: (i, 0))
    return pl.pallas_call(
        _merge_kernel,
        grid=(n // tm,),
        in_specs=[wide, wide, wide, half, half, half, half, half, half,
                  _full((ATT_KV_WIDTH, D_MODEL)), _full((D_MODEL, D_MODEL)), _full((1, D_MODEL))],
        out_specs=[wide, wide],
        out_shape=[jax.ShapeDtypeStruct((n, D_MODEL), F32), jax.ShapeDtypeStruct((n, D_MODEL), F32)],
        compiler_params=_params("parallel"),
        name="merge",
    )(x2d, pa, gb, *os_, *ls_, wb_bf16, wo_bf16, g)


def _topk_rows(s, k, payload=None):
    r = s.shape[0]
    rows = lax.broadcasted_iota(I32, s.shape, 0)
    vals, ids = [], []
    for _ in range(k):
        m = jnp.max(s, axis=0, keepdims=True)
        am = jnp.min(jnp.where(s == m, rows, r), axis=0, keepdims=True)
        hit = rows == am
        vals.append(m)
        if payload is None:
            ids.append(am)
        else:
            ids.append(jnp.max(jnp.where(hit, payload, -1), axis=0, keepdims=True))
        s = jnp.where(hit, -jnp.inf, s)
    return jnp.concatenate(vals, axis=0), jnp.concatenate(ids, axis=0)


def _split_bf16(a):
    hi = a.astype(BF16)
    return hi, (a - hi.astype(F32)).astype(BF16)


def _peer_topk_kernel(h_ref, wq_ref, skh_ref, skl_ref, idx_ref, gate_ref):
    q = jnp.dot(h_ref[...].astype(BF16), wq_ref[...], preferred_element_type=F32)
    nt = (((1,), (1,)), ((), ()))
    idx_rows, gate_rows = [], []
    for hd in range(PEER_HEADS):
        top_s, top_i = [], []
        for p in range(2):
            hp = hd * 2 + p
            q_hi, q_lo = _split_bf16(q[:, hp * PEER_HALF:(hp + 1) * PEER_HALF])
            kh, kl = skh_ref[hp], skl_ref[hp]
            st = (lax.dot_general(kh, q_hi, nt, preferred_element_type=F32)
                  + lax.dot_general(kh, q_lo, nt, preferred_element_type=F32)
                  + lax.dot_general(kl, q_hi, nt, preferred_element_type=F32))
            ts, ti = _topk_rows(st, PEER_TOPK)
            top_s.append(ts)
            top_i.append(ti)
        cand = jnp.concatenate([top_s[0][a:a + 1] + top_s[1] for a in range(PEER_TOPK)], axis=0)
        cidx = jnp.concatenate([top_i[0][a:a + 1] * PEER_N_KEYS + top_i[1] for a in range(PEER_TOPK)], axis=0)
        best_s, expert = _topk_rows(cand, PEER_TOPK, payload=cidx)
        e = jnp.exp(best_s - best_s[0:1])
        gate_rows.append(e / jnp.sum(e, axis=0, keepdims=True))
        idx_rows.append(expert)
    idx_ref[...] = jnp.concatenate(idx_rows, axis=0).T
    gate_ref[...] = jnp.concatenate(gate_rows, axis=0).T


def _peer_topk(h2, wq_bf16, sk_hi, sk_lo, tm=256):
    n = h2.shape[0]
    qw = wq_bf16.shape[1]
    out = pl.BlockSpec((tm, PEER_PICKS), lambda i: (i, 0))
    return pl.pallas_call(
        _peer_topk_kernel,
        grid=(n // tm,),
        in_specs=[pl.BlockSpec((tm, D_MODEL), lambda i: (i, 0)),
                  _full((D_MODEL, qw)),
                  _full((2 * PEER_HEADS, PEER_N_KEYS, PEER_HALF)),
                  _full((2 * PEER_HEADS, PEER_N_KEYS, PEER_HALF))],
        out_specs=[out, out],
        out_shape=[jax.ShapeDtypeStruct((n, PEER_PICKS), I32), jax.ShapeDtypeStruct((n, PEER_PICKS), F32)],
        compiler_params=_params("parallel"),
        name="peer_topk",
    )(h2, wq_bf16, sk_hi, sk_lo)


def _pack_table(tab):
    bits = lax.bitcast_convert_type(tab.astype(BF16), jnp.uint16).astype(jnp.uint32)
    word = bits[:, :ROW_WORDS] | (bits[:, ROW_WORDS:] << 16)
    return lax.bitcast_convert_type(word, I32).reshape(tab.shape[0] * ROW_SUBLANES, LANES)


def _gather_row(tab_ref, e):
    w = tab_ref[pl.ds(pl.multiple_of(e * ROW_SUBLANES, ROW_SUBLANES), ROW_SUBLANES), :]
    lo = lax.bitcast_convert_type(w << 16, F32)
    hi = lax.bitcast_convert_type(w & np.int32(-65536), F32)
    return lo, hi


TOKENS_PER_GROUP = SUBLANES


def _peer_u_kernel(idx_ref, h_ref, gate_ref, tab_ref, act_ref, slot_ref):
    tm = gate_ref.shape[0]
    slot_rows = PEER_PICKS * ROW_SUBLANES
    ones = jnp.ones((LANES, LANES), BF16)
    diag = (lax.broadcasted_iota(I32, (PEER_PICKS, LANES), 0)
            == lax.broadcasted_iota(I32, (PEER_PICKS, LANES), 1))

    def group(g, carry):
        def token(tt, carry2):
            t = g * TOKENS_PER_GROUP + tt
            hv = h_ref[pl.ds(pl.multiple_of(t * SUBLANES, SUBLANES), SUBLANES), :]
            h_lo, h_hi = hv[0:ROW_SUBLANES], hv[ROW_SUBLANES:SUBLANES]
            base = pl.multiple_of(tt * slot_rows, slot_rows)
            for k in range(PEER_PICKS):
                lo, hi = _gather_row(tab_ref, idx_ref[t, k])
                slot_ref[pl.ds(base + k * ROW_SUBLANES, ROW_SUBLANES), :] = lo * h_lo + hi * h_hi
            return carry2

        lax.fori_loop(0, TOKENS_PER_GROUP, token, 0)
        n_rows = TOKENS_PER_GROUP * PEER_PICKS
        r = slot_ref[pl.ds(0, n_rows, stride=ROW_SUBLANES), :]
        for s in range(1, ROW_SUBLANES):
            r = r + slot_ref[pl.ds(s, n_rows, stride=ROW_SUBLANES), :]
        r_hi, r_lo = _split_bf16(r)
        rs = (jnp.dot(r_hi, ones, preferred_element_type=F32)
              + jnp.dot(r_lo, ones, preferred_element_type=F32))
        rows = [jnp.sum(jnp.where(diag, rs[tt * PEER_PICKS:(tt + 1) * PEER_PICKS], 0.0), axis=0, keepdims=True)
                for tt in range(TOKENS_PER_GROUP)]
        a = jnp.concatenate(rows, axis=0)
        off = pl.multiple_of(g * TOKENS_PER_GROUP, TOKENS_PER_GROUP)
        act_ref[pl.ds(off, TOKENS_PER_GROUP), :] = _gelu(a) * gate_ref[pl.ds(off, TOKENS_PER_GROUP), :]
        return carry

    lax.fori_loop(0, tm // TOKENS_PER_GROUP, group, 0)


def _peer_u(idx, h2, gate, tab, tm=128):
    n = idx.shape[0]
    return pl.pallas_call(
        _peer_u_kernel,
        grid=(n // tm,),
        in_specs=[pl.BlockSpec((tm, PEER_PICKS), lambda i: (i, 0), memory_space=pltpu.SMEM),
                  pl.BlockSpec((tm * SUBLANES, LANES), lambda i: (i, 0)),
                  pl.BlockSpec((tm, PEER_PICKS), lambda i: (i, 0)),
                  pl.BlockSpec(memory_space=pltpu.VMEM)],
        out_specs=pl.BlockSpec((tm, PEER_PICKS), lambda i: (i, 0)),
        out_shape=jax.ShapeDtypeStruct((n, PEER_PICKS), F32),
        scratch_shapes=[pltpu.VMEM((TOKENS_PER_GROUP * PEER_PICKS * ROW_SUBLANES, LANES), F32)],
        compiler_params=_params("arbitrary"),
        name="peer_u",
    )(idx, h2.reshape(n * SUBLANES, LANES), gate, tab)


N_ACC = 4


def _peer_v_kernel(idx_ref, act_ref, x_ref, tab_ref, out_ref):
    tm = idx_ref.shape[0]

    def token(t, carry):
        acc_lo = [jnp.zeros((ROW_SUBLANES, LANES), F32) for _ in range(N_ACC)]
        acc_hi = [jnp.zeros((ROW_SUBLANES, LANES), F32) for _ in range(N_ACC)]
        for k in range(PEER_PICKS):
            lo, hi = _gather_row(tab_ref, idx_ref[t, k])
            a = act_ref[t, k]
            acc_lo[k % N_ACC] = acc_lo[k % N_ACC] + a * lo
            acc_hi[k % N_ACC] = acc_hi[k % N_ACC] + a * hi
        lo = (acc_lo[0] + acc_lo[1]) + (acc_lo[2] + acc_lo[3])
        hi = (acc_hi[0] + acc_hi[1]) + (acc_hi[2] + acc_hi[3])
        r0 = pl.multiple_of(t * SUBLANES, SUBLANES)
        r1 = pl.multiple_of(t * SUBLANES + ROW_SUBLANES, ROW_SUBLANES)
        out_ref[pl.ds(r0, ROW_SUBLANES), :] = x_ref[pl.ds(r0, ROW_SUBLANES), :] + lo
        out_ref[pl.ds(r1, ROW_SUBLANES), :] = x_ref[pl.ds(r1, ROW_SUBLANES), :] + hi
        return carry

    lax.fori_loop(0, tm, token, 0)


def _peer_v(idx, act, x1, tab, tm=128):
    n = idx.shape[0]
    smem = pl.BlockSpec((tm, PEER_PICKS), lambda i: (i, 0), memory_space=pltpu.SMEM)
    rows = pl.BlockSpec((tm * SUBLANES, LANES), lambda i: (i, 0))
    out = pl.pallas_call(
        _peer_v_kernel,
        grid=(n // tm,),
        in_specs=[smem, smem, rows, pl.BlockSpec(memory_space=pltpu.VMEM)],
        out_specs=rows,
        out_shape=jax.ShapeDtypeStruct((n * SUBLANES, LANES), F32),
        compiler_params=_params("arbitrary"),
        name="peer_v",
    )(idx, act, x1.reshape(n * SUBLANES, LANES), tab)
    return out.reshape(n, D_MODEL)


def _final_kernel(x_ref, g_ref, o_ref):
    o_ref[...] = _rms(x_ref[...], g_ref[...])


def _final_norm(x2, g, tm=1024):
    n = x2.shape[0]
    row = pl.BlockSpec((tm, D_MODEL), lambda i: (i, 0))
    return pl.pallas_call(
        _final_kernel,
        grid=(n // tm,),
        in_specs=[row, _full((1, D_MODEL))],
        out_specs=row,
        out_shape=jax.ShapeDtypeStruct((n, D_MODEL), F32),
        compiler_params=_params("parallel"),
        name="final_norm",
    )(x2, g)


def kernel(x, norm_mix_g, w_in, sgu_norm_g, sgu_w, sgu_b, w_branch_a, w_branch_b, w_out,
           norm_ffn_g, peer_wq, peer_subkeys, peer_u, peer_v, norm_final_g):
    b, s, d = x.shape
    n = b * s
    x2 = x.reshape(n, d)
    for l in range(w_in.shape[0]):
        u, v, q, k, vv, ga, gb = _in_proj(x2, norm_mix_g[l].reshape(1, d), w_in[l].astype(BF16))
        bs_b = jnp.broadcast_to(sgu_b[l][:, :, None], (SGU_GROUPS, SGU_CHUNK, LANES))
        pa = _sgu(u, v, ga, sgu_norm_g[l].reshape(1, SGU_WIDTH), sgu_w[l].astype(BF16), bs_b,
                  w_branch_a[l].astype(BF16))
        q3, k3, v3 = (t.reshape(b, s, -1) for t in (q, k, vv))
        os_, ls_ = [], []
        for g, (window, dilation) in enumerate(ATT_PATTERNS):
            o, lse = _attention_pattern(q3, k3, v3, g, dilation, window // (2 * dilation))
            os_.append(o)
            ls_.append(lse)
        x1, h2 = _merge(x2, pa, gb, os_, ls_, w_branch_b[l].astype(BF16), w_out[l].astype(BF16),
                        norm_ffn_g[l].reshape(1, d))
        sk = peer_subkeys[l].reshape(2 * PEER_HEADS, PEER_N_KEYS, PEER_HALF)
        sk_hi = sk.astype(BF16)
        sk_lo = (sk - sk_hi.astype(F32)).astype(BF16)
        idx, gate = _peer_topk(h2, peer_wq[l].astype(BF16), sk_hi, sk_lo)
        act = _peer_u(idx, h2, gate, _pack_table(peer_u[l]))
        x2 = _peer_v(idx, act, x1, _pack_table(peer_v[l]))
    return _final_norm(x2, norm_final_g.reshape(1, d)).reshape(b, s, d)
```

```python
import functools

import numpy as np
import jax
import jax.numpy as jnp
from jax import lax
from jax.experimental import pallas as pl
from jax.experimental.pallas import tpu as pltpu

F32 = jnp.float32
BF16 = jnp.bfloat16
I32 = jnp.int32

D_MODEL = 1024
EPS = 1e-6
NEG_INF = -1e30
SGU_CHUNK = 128
SGU_GROUPS = 8
SGU_WIDTH = 1024
ATT_HEADS = 8
ATT_HEAD_DIM = 64
ATT_PATTERNS = ((128, 1), (512, 4), (2048, 16))
ATT_GROUPS = 3
ATT_KV_WIDTH = ATT_HEADS * ATT_HEAD_DIM
ATT_Q_WIDTH = ATT_GROUPS * ATT_KV_WIDTH
PEER_HEADS = 8
PEER_N_KEYS = 128
PEER_N_EXPERTS = PEER_N_KEYS * PEER_N_KEYS
PEER_HALF = 128
PEER_TOPK = 16
PEER_PICKS = PEER_HEADS * PEER_TOPK

LANES = 128
SUBLANES = 8
ROW_WORDS = D_MODEL // 2
ROW_SUBLANES = ROW_WORDS // LANES
VMEM_LIMIT = 56 * 1024 * 1024

IN_CHUNK = 512
_IN_LAYOUT = (
    (0, 0, "gelu"), (0, 1, "gelu"), (1, 0, "gelu"), (1, 1, "gelu"),
    (2, 0, None), (2, 1, None), (2, 2, None), (3, 0, None), (4, 0, None),
    (5, 0, "sigmoid"), (5, 1, "sigmoid"), (6, 0, "sigmoid"), (6, 1, "sigmoid"),
)
_IN_WIDTHS = (1024, 1024, 1536, 512, 512, 1024, 1024)


def _params(*sem):
    return pltpu.CompilerParams(dimension_semantics=sem, vmem_limit_bytes=VMEM_LIMIT)


def _gelu(y):
    return 0.5 * y * (1.0 + lax.erf(y * np.float32(1.0 / np.sqrt(2.0))))


def _rms(x, g):
    return x * lax.rsqrt(jnp.mean(x * x, axis=-1, keepdims=True) + EPS) * g


def _full(shape):
    return pl.BlockSpec(shape, lambda *_: (0,) * len(shape))


def _in_proj_kernel(x_ref, g_ref, w_ref, *out_refs):
    h = _rms(x_ref[...], g_ref[...]).astype(BF16)
    for c, (slot, sub, epi) in enumerate(_IN_LAYOUT):
        y = jnp.dot(h, w_ref[:, c * IN_CHUNK:(c + 1) * IN_CHUNK], preferred_element_type=F32)
        if epi == "gelu":
            y = _gelu(y)
        elif epi == "sigmoid":
            y = jax.nn.sigmoid(y)
        out_refs[slot][:, sub * IN_CHUNK:(sub + 1) * IN_CHUNK] = y.astype(BF16)


def _in_proj(x2d, g, w_bf16, tm=512):
    n = x2d.shape[0]
    width = w_bf16.shape[1]
    return pl.pallas_call(
        _in_proj_kernel,
        grid=(n // tm,),
        in_specs=[pl.BlockSpec((tm, D_MODEL), lambda i: (i, 0)),
                  _full((1, D_MODEL)),
                  pl.BlockSpec((D_MODEL, width), lambda i: (0, 0), pipeline_mode=pl.Buffered(1))],
        out_specs=[pl.BlockSpec((tm, w), lambda i: (i, 0)) for w in _IN_WIDTHS],
        out_shape=[jax.ShapeDtypeStruct((n, w), BF16) for w in _IN_WIDTHS],
        compiler_params=_params("parallel"),
        name="in_proj",
    )(x2d, g, w_bf16)


def _sgu_kernel(u_ref, v_ref, ga_ref, ng_ref, ws_ref, bs_ref, wa_ref, out_ref):
    tm = u_ref.shape[0]
    vn = _rms(v_ref[...].astype(F32), ng_ref[...]).astype(BF16)
    chunks = []
    for c in range(tm // SGU_CHUNK):
        cols = []
        for g in range(SGU_GROUPS):
            blk = vn[c * SGU_CHUNK:(c + 1) * SGU_CHUNK, g * LANES:(g + 1) * LANES]
            cols.append(jnp.dot(ws_ref[g], blk, preferred_element_type=F32) + bs_ref[g])
        chunks.append(jnp.concatenate(cols, axis=1))
    mixed = jnp.concatenate(chunks, axis=0)
    ya = (u_ref[...].astype(F32) * mixed).astype(BF16)
    pa = jnp.dot(ya, wa_ref[...], preferred_element_type=F32) * ga_ref[...].astype(F32)
    out_ref[...] = pa.astype(BF16)


def _sgu(u, v, ga, ng, ws_bf16, bs_b, wa_bf16, tm=512):
    n = u.shape[0]
    row = pl.BlockSpec((tm, SGU_WIDTH), lambda i: (i, 0))
    return pl.pallas_call(
        _sgu_kernel,
        grid=(n // tm,),
        in_specs=[row, row, row, _full((1, SGU_WIDTH)),
                  _full((SGU_GROUPS, SGU_CHUNK, SGU_CHUNK)),
                  _full((SGU_GROUPS, SGU_CHUNK, LANES)),
                  _full((SGU_WIDTH, D_MODEL))],
        out_specs=pl.BlockSpec((tm, D_MODEL), lambda i: (i, 0)),
        out_shape=jax.ShapeDtypeStruct((n, D_MODEL), BF16),
        compiler_params=_params("parallel"),
        name="sgu",
    )(u, v, ga, ng, ws_bf16, bs_b, wa_bf16)


def _attn_kernel(q_ref, kp_ref, kc_ref, kn_ref, vp_ref, vc_ref, vn_ref, o_ref, lse_ref,
                 *, dilation, n_side, sub_len):
    tq = q_ref.shape[0]
    win = tq + 2 * n_side
    i = pl.program_id(2)
    k = jnp.concatenate([kp_ref[...], kc_ref[...], kn_ref[...]], axis=0)
    v = jnp.concatenate([vp_ref[...], vc_ref[...], vn_ref[...]], axis=0)
    row = lax.broadcasted_iota(I32, (tq, win), 0)
    col = lax.broadcasted_iota(I32, (tq, win), 1)
    rel = col - n_side - row
    pos = i * tq - n_side + col
    valid = (jnp.abs(rel) <= n_side) & (pos >= 0) & (pos < sub_len)
    negdist = -(jnp.abs(rel) * dilation).astype(F32)
    lane = lax.broadcasted_iota(I32, (tq, LANES), 1)
    low_half = lane < ATT_HEAD_DIM
    scale = np.float32(ATT_HEAD_DIM ** -0.5)
    for pair in range(ATT_HEADS // 2):
        sl = slice(pair * LANES, (pair + 1) * LANES)
        qp, kpair, vpair = q_ref[:, sl], k[:, sl], v[:, sl]
        outs, lses = [], []
        for j in range(2):
            slope = np.float32(2.0 ** (-8.0 * (pair * 2 + j + 1) / ATT_HEADS))
            qm = jnp.where(low_half if j == 0 else jnp.logical_not(low_half), qp, jnp.zeros_like(qp))
            s = lax.dot_general(qm, kpair, (((1,), (1,)), ((), ())), preferred_element_type=F32)
            s = jnp.where(valid, s * scale + slope * negdist, NEG_INF)
            m = jnp.max(s, axis=-1, keepdims=True)
            p = jnp.exp(s - m)
            l = jnp.sum(p, axis=-1, keepdims=True)
            o = jnp.dot(p.astype(BF16), vpair, preferred_element_type=F32)
            outs.append(o / l)
            lses.append(jnp.broadcast_to(m + jnp.log(l), (tq, LANES)))
        o_ref[:, sl] = jnp.where(low_half, outs[0], outs[1]).astype(o_ref.dtype)
        lse_ref[:, sl] = jnp.where(low_half, lses[0], lses[1])


def _attention_pattern(q3, k3, v3, group, dilation, n_side, tq=128):
    b, s, _ = q3.shape
    sub_len = s // dilation
    kvw = ATT_KV_WIDTH
    qv = q3.reshape(b, sub_len, dilation * ATT_Q_WIDTH)
    kv = k3.reshape(b, sub_len, dilation * kvw)
    vv = v3.reshape(b, sub_len, dilation * kvw)
    per = tq // n_side
    last = sub_len // n_side - 1
    cur = pl.BlockSpec((None, tq, kvw), lambda bi, r, i: (bi, i, r))
    prev = pl.BlockSpec((None, n_side, kvw), lambda bi, r, i: (bi, jnp.maximum(i * per - 1, 0), r))
    nxt = pl.BlockSpec((None, n_side, kvw), lambda bi, r, i: (bi, jnp.minimum((i + 1) * per, last), r))
    o, lse = pl.pallas_call(
        functools.partial(_attn_kernel, dilation=dilation, n_side=n_side, sub_len=sub_len),
        grid=(b, dilation, sub_len // tq),
        in_specs=[pl.BlockSpec((None, tq, kvw), lambda bi, r, i: (bi, i, r * ATT_GROUPS + group)),
                  prev, cur, nxt, prev, cur, nxt],
        out_specs=[cur, cur],
        out_shape=[jax.ShapeDtypeStruct((b, sub_len, dilation * kvw), BF16),
                   jax.ShapeDtypeStruct((b, sub_len, dilation * kvw), F32)],
        compiler_params=_params("parallel", "parallel", "parallel"),
        name=f"attn_d{dilation}",
    )(qv, kv, kv, kv, vv, vv, vv)
    return o.reshape(b * s, kvw), lse.reshape(b * s, kvw)


def _merge_kernel(x_ref, pa_ref, gb_ref, o1_ref, o2_ref, o3_ref, l1_ref, l2_ref, l3_ref,
                  wb_ref, wo_ref, g_ref, x1_ref, h2_ref):
    l1, l2, l3 = l1_ref[...], l2_ref[...], l3_ref[...]
    m = jnp.maximum(jnp.maximum(l1, l2), l3)
    w1, w2, w3 = jnp.exp(l1 - m), jnp.exp(l2 - m), jnp.exp(l3 - m)
    yb = (w1 * o1_ref[...].astype(F32) + w2 * o2_ref[...].astype(F32)
          + w3 * o3_ref[...].astype(F32)) / (w1 + w2 + w3)
    pb = jnp.dot(yb.astype(BF16), wb_ref[...], preferred_element_type=F32) * gb_ref[...].astype(F32)
    merged = (pa_ref[...].astype(F32) + pb).astype(BF16)
    x1 = x_ref[...] + jnp.dot(merged, wo_ref[...], preferred_element_type=F32)
    x1_ref[...] = x1
    h2_ref[...] = _rms(x1, g_ref[...])


def _merge(x2d, pa, gb, os_, ls_, wb_bf16, wo_bf16, g, tm=512):
    n = x2d.shape[0]
    wide = pl.BlockSpec((tm, D_MODEL), lambda i: (i, 0))
    half = pl.BlockSpec((tm, ATT_KV_WIDTH), lambda i: (i, 0))
    return pl.pallas_call(
        _merge_kernel,
        grid=(n // tm,),
        in_specs=[wide, wide, wide, half, half, half, half, half, half,
                  _full((ATT_KV_WIDTH, D_MODEL)), _full((D_MODEL, D_MODEL)), _full((1, D_MODEL))],
        out_specs=[wide, wide],
        out_shape=[jax.ShapeDtypeStruct((n, D_MODEL), F32), jax.ShapeDtypeStruct((n, D_MODEL), F32)],
        compiler_params=_params("parallel"),
        name="merge",
    )(x2d, pa, gb, *os_, *ls_, wb_bf16, wo_bf16, g)


def _topk_rows(s, k, payload=None):
    r = s.shape[0]
    rows = lax.broadcasted_iota(I32, s.shape, 0)
    vals, ids = [], []
    for _ in range(k):
        m = jnp.max(s, axis=0, keepdims=True)
        am = jnp.min(jnp.where(s == m, rows, r), axis=0, keepdims=True)
        hit = rows == am
        vals.append(m)
        if payload is None:
            ids.append(am)
        else:
            ids.append(jnp.max(jnp.where(hit, payload, -1), axis=0, keepdims=True))
        s = jnp.where(hit, -jnp.inf, s)
    return jnp.concatenate(vals, axis=0), jnp.concatenate(ids, axis=0)


def _split_bf16(a):
    hi = a.astype(BF16)
    return hi, (a - hi.astype(F32)).astype(BF16)


def _peer_topk_kernel(h_ref, wq_ref, skh_ref, skl_ref, idx_ref, gate_ref):
    q = jnp.dot(h_ref[...].astype(BF16), wq_ref[...], preferred_element_type=F32)
    nt = (((1,), (1,)), ((), ()))
    idx_rows, gate_rows = [], []
    for hd in range(PEER_HEADS):
        top_s, top_i = [], []
        for p in range(2):
            hp = hd * 2 + p
            q_hi, q_lo = _split_bf16(q[:, hp * PEER_HALF:(hp + 1) * PEER_HALF])
            kh, kl = skh_ref[hp], skl_ref[hp]
            st = (lax.dot_general(kh, q_hi, nt, preferred_element_type=F32)
                  + lax.dot_general(kh, q_lo, nt, preferred_element_type=F32)
                  + lax.dot_general(kl, q_hi, nt, preferred_element_type=F32))
            ts, ti = _topk_rows(st, PEER_TOPK)
            top_s.append(ts)
            top_i.append(ti)
        cand = jnp.concatenate([top_s[0][a:a + 1] + top_s[1] for a in range(PEER_TOPK)], axis=0)
        cidx = jnp.concatenate([top_i[0][a:a + 1] * PEER_N_KEYS + top_i[1] for a in range(PEER_TOPK)], axis=0)
        best_s, expert = _topk_rows(cand, PEER_TOPK, payload=cidx)
        e = jnp.exp(best_s - best_s[0:1])
        gate_rows.append(e / jnp.sum(e, axis=0, keepdims=True))
        idx_rows.append(expert * ROW_SUBLANES)
    idx_ref[...] = jnp.concatenate(idx_rows, axis=0).T
    gate_ref[...] = jnp.concatenate(gate_rows, axis=0).T


def _peer_topk(h2, wq_bf16, sk_hi, sk_lo, tm=256):
    n = h2.shape[0]
    qw = wq_bf16.shape[1]
    out = pl.BlockSpec((tm, PEER_PICKS), lambda i: (i, 0))
    return pl.pallas_call(
        _peer_topk_kernel,
        grid=(n // tm,),
        in_specs=[pl.BlockSpec((tm, D_MODEL), lambda i: (i, 0)),
                  _full((D_MODEL, qw)),
                  _full((2 * PEER_HEADS, PEER_N_KEYS, PEER_HALF)),
                  _full((2 * PEER_HEADS, PEER_N_KEYS, PEER_HALF))],
        out_specs=[out, out],
        out_shape=[jax.ShapeDtypeStruct((n, PEER_PICKS), I32), jax.ShapeDtypeStruct((n, PEER_PICKS), F32)],
        compiler_params=_params("parallel"),
        name="peer_topk",
    )(h2, wq_bf16, sk_hi, sk_lo)


def _pack_table(tab):
    lo16 = lax.bitcast_convert_type(tab[:, :ROW_WORDS].astype(BF16), jnp.uint16).astype(jnp.uint32)
    full = lax.bitcast_convert_type(tab[:, ROW_WORDS:], jnp.uint32)
    sign = full & jnp.uint32(0x80000000)
    mag = (full & jnp.uint32(0x7FFFFFFF)) + jnp.uint32(0x8000)
    hi16 = jnp.where(mag >= lo16, (mag - lo16) >> 16, jnp.uint32(0))
    word = sign | (hi16 << 16) | lo16
    return lax.bitcast_convert_type(word, I32).reshape(tab.shape[0] * ROW_SUBLANES, LANES)


def _gather_row(tab_ref, e4):
    w = tab_ref[pl.ds(pl.multiple_of(e4, ROW_SUBLANES), ROW_SUBLANES), :]
    return lax.bitcast_convert_type(w << 16, F32), lax.bitcast_convert_type(w, F32)


def _row_sums_all_lanes(rows, ones):
    r_hi, r_lo = _split_bf16(rows)
    return (jnp.dot(r_hi, ones, preferred_element_type=F32)
            + jnp.dot(r_lo, ones, preferred_element_type=F32))


TOKENS_PER_GROUP = SUBLANES


def _peer_u_kernel(idx_ref, h_ref, gate_ref, tab_ref, act_ref, slot_ref):
    tm = gate_ref.shape[0]
    slot_rows = PEER_PICKS * ROW_SUBLANES
    ones = jnp.ones((LANES, LANES), BF16)
    diag = (lax.broadcasted_iota(I32, (PEER_PICKS, LANES), 0)
            == lax.broadcasted_iota(I32, (PEER_PICKS, LANES), 1))

    def group(g, carry):
        def token(tt, carry2):
            t = g * TOKENS_PER_GROUP + tt
            hv = h_ref[pl.ds(pl.multiple_of(t * SUBLANES, SUBLANES), SUBLANES), :]
            h_lo, h_hi = hv[0:ROW_SUBLANES], hv[ROW_SUBLANES:SUBLANES]
            base = pl.multiple_of(tt * slot_rows, slot_rows)
            for k in range(PEER_PICKS):
                lo, hi = _gather_row(tab_ref, idx_ref[t, k])
                slot_ref[pl.ds(base + k * ROW_SUBLANES, ROW_SUBLANES), :] = lo * h_lo + hi * h_hi
            return carry2

        lax.fori_loop(0, TOKENS_PER_GROUP, token, 0)
        n_rows = TOKENS_PER_GROUP * PEER_PICKS
        r = slot_ref[pl.ds(0, n_rows, stride=ROW_SUBLANES), :]
        for s in range(1, ROW_SUBLANES):
            r = r + slot_ref[pl.ds(s, n_rows, stride=ROW_SUBLANES), :]
        rs = _row_sums_all_lanes(r, ones)
        rows = [jnp.sum(jnp.where(diag, rs[tt * PEER_PICKS:(tt + 1) * PEER_PICKS], 0.0), axis=0, keepdims=True)
                for tt in range(TOKENS_PER_GROUP)]
        a = jnp.concatenate(rows, axis=0)
        off = pl.multiple_of(g * TOKENS_PER_GROUP, TOKENS_PER_GROUP)
        act_ref[pl.ds(off, TOKENS_PER_GROUP), :] = _gelu(a) * gate_ref[pl.ds(off, TOKENS_PER_GROUP), :]
        return carry

    lax.fori_loop(0, tm // TOKENS_PER_GROUP, group, 0)


def _peer_u(idx, h2, gate, tab, tm=128):
    n = idx.shape[0]
    return pl.pallas_call(
        _peer_u_kernel,
        grid=(n // tm,),
        in_specs=[pl.BlockSpec((tm, PEER_PICKS), lambda i: (i, 0), memory_space=pltpu.SMEM),
                  pl.BlockSpec((tm * SUBLANES, LANES), lambda i: (i, 0)),
                  pl.BlockSpec((tm, PEER_PICKS), lambda i: (i, 0)),
                  pl.BlockSpec(memory_space=pltpu.VMEM)],
        out_specs=pl.BlockSpec((tm, PEER_PICKS), lambda i: (i, 0)),
        out_shape=jax.ShapeDtypeStruct((n, PEER_PICKS), F32),
        scratch_shapes=[pltpu.VMEM((TOKENS_PER_GROUP * PEER_PICKS * ROW_SUBLANES, LANES), F32)],
        compiler_params=_params("arbitrary"),
        name="peer_u",
    )(idx, h2.reshape(n * SUBLANES, LANES), gate, tab)


N_ACC = 4


def _peer_v_kernel(idx_ref, act_ref, x_ref, tab_ref, out_ref, actb_ref):
    tm = idx_ref.shape[0]
    ones = jnp.ones((LANES, LANES), BF16)
    diag = (lax.broadcasted_iota(I32, (PEER_PICKS, LANES), 0)
            == lax.broadcasted_iota(I32, (PEER_PICKS, LANES), 1))

    def group(g, carry):
        off = pl.multiple_of(g * TOKENS_PER_GROUP, TOKENS_PER_GROUP)
        a8 = act_ref[pl.ds(off, TOKENS_PER_GROUP), :]
        spread = jnp.concatenate([jnp.where(diag, a8[tt:tt + 1, :], 0.0) for tt in range(TOKENS_PER_GROUP)], axis=0)
        actb_ref[...] = _row_sums_all_lanes(spread, ones)

        def token(tt, carry2):
            t = g * TOKENS_PER_GROUP + tt
            base = pl.multiple_of(tt * PEER_PICKS, PEER_PICKS)
            acc_lo = [jnp.zeros((ROW_SUBLANES, LANES), F32) for _ in range(N_ACC)]
            acc_hi = [jnp.zeros((ROW_SUBLANES, LANES), F32) for _ in range(N_ACC)]
            for k in range(PEER_PICKS):
                lo, hi = _gather_row(tab_ref, idx_ref[t, k])
                a = actb_ref[pl.ds(base + k, 1), :]
                acc_lo[k % N_ACC] = acc_lo[k % N_ACC] + a * lo
                acc_hi[k % N_ACC] = acc_hi[k % N_ACC] + a * hi
            lo = (acc_lo[0] + acc_lo[1]) + (acc_lo[2] + acc_lo[3])
            hi = (acc_hi[0] + acc_hi[1]) + (acc_hi[2] + acc_hi[3])
            r0 = pl.multiple_of(t * SUBLANES, SUBLANES)
            r1 = pl.multiple_of(t * SUBLANES + ROW_SUBLANES, ROW_SUBLANES)
            out_ref[pl.ds(r0, ROW_SUBLANES), :] = x_ref[pl.ds(r0, ROW_SUBLANES), :] + lo
            out_ref[pl.ds(r1, ROW_SUBLANES), :] = x_ref[pl.ds(r1, ROW_SUBLANES), :] + hi
            return carry2

        lax.fori_loop(0, TOKENS_PER_GROUP, token, 0)
        return carry

    lax.fori_loop(0, tm // TOKENS_PER_GROUP, group, 0)


def _peer_v(idx, act, x1, tab, tm=128):
    n = idx.shape[0]
    rows = pl.BlockSpec((tm * SUBLANES, LANES), lambda i: (i, 0))
    out = pl.pallas_call(
        _peer_v_kernel,
        grid=(n // tm,),
        in_specs=[pl.BlockSpec((tm, PEER_PICKS), lambda i: (i, 0), memory_space=pltpu.SMEM),
                  pl.BlockSpec((tm, PEER_PICKS), lambda i: (i, 0)),
                  rows, pl.BlockSpec(memory_space=pltpu.VMEM)],
        out_specs=rows,
        out_shape=jax.ShapeDtypeStruct((n * SUBLANES, LANES), F32),
        scratch_shapes=[pltpu.VMEM((TOKENS_PER_GROUP * PEER_PICKS, LANES), F32)],
        compiler_params=_params("arbitrary"),
        name="peer_v",
    )(idx, act, x1.reshape(n * SUBLANES, LANES), tab)
    return out.reshape(n, D_MODEL)


def _final_kernel(x_ref, g_ref, o_ref):
    o_ref[...] = _rms(x_ref[...], g_ref[...])


def _final_norm(x2, g, tm=1024):
    n = x2.shape[0]
    row = pl.BlockSpec((tm, D_MODEL), lambda i: (i, 0))
    return pl.pallas_call(
        _final_kernel,
        grid=(n // tm,),
        in_specs=[row, _full((1, D_MODEL))],
        out_specs=row,
        out_shape=jax.ShapeDtypeStruct((n, D_MODEL), F32),
        compiler_params=_params("parallel"),
        name="final_norm",
    )(x2, g)


def kernel(x, norm_mix_g, w_in, sgu_norm_g, sgu_w, sgu_b, w_branch_a, w_branch_b, w_out,
           norm_ffn_g, peer_wq, peer_subkeys, peer_u, peer_v, norm_final_g):
    b, s, d = x.shape
    n = b * s
    x2 = x.reshape(n, d)
    for l in range(w_in.shape[0]):
        u, v, q, k, vv, ga, gb = _in_proj(x2, norm_mix_g[l].reshape(1, d), w_in[l].astype(BF16))
        bs_b = jnp.broadcast_to(sgu_b[l][:, :, None], (SGU_GROUPS, SGU_CHUNK, LANES))
        pa = _sgu(u, v, ga, sgu_norm_g[l].reshape(1, SGU_WIDTH), sgu_w[l].astype(BF16), bs_b,
                  w_branch_a[l].astype(BF16))
        q3, k3, v3 = (t.reshape(b, s, -1) for t in (q, k, vv))
        os_, ls_ = [], []
        for g, (window, dilation) in enumerate(ATT_PATTERNS):
            o, lse = _attention_pattern(q3, k3, v3, g, dilation, window // (2 * dilation))
            os_.append(o)
            ls_.append(lse)
        x1, h2 = _merge(x2, pa, gb, os_, ls_, w_branch_b[l].astype(BF16), w_out[l].astype(BF16),
                        norm_ffn_g[l].reshape(1, d))
        sk = peer_subkeys[l].reshape(2 * PEER_HEADS, PEER_N_KEYS, PEER_HALF)
        sk_hi = sk.astype(BF16)
        sk_lo = (sk - sk_hi.astype(F32)).astype(BF16)
        idx, gate = _peer_topk(h2, peer_wq[l].astype(BF16), sk_hi, sk_lo)
        act = _peer_u(idx, h2, gate, _pack_table(peer_u[l]))
        x2 = _peer_v(idx, act, x1, _pack_table(peer_v[l]))
    return _final_norm(x2, norm_final_g.reshape(1, d)).reshape(b, s, d)
```

```python
import functools

import numpy as np
import jax
import jax.numpy as jnp
from jax import lax
from jax.experimental import pallas as pl
from jax.experimental.pallas import tpu as pltpu

F32 = jnp.float32
BF16 = jnp.bfloat16
I32 = jnp.int32

D_MODEL = 1024
EPS = 1e-6
NEG_INF = -1e30
SGU_CHUNK = 128
SGU_GROUPS = 8
SGU_WIDTH = 1024
ATT_HEADS = 8
ATT_HEAD_DIM = 64
ATT_PATTERNS = ((128, 1), (512, 4), (2048, 16))
ATT_GROUPS = 3
ATT_KV_WIDTH = ATT_HEADS * ATT_HEAD_DIM
ATT_Q_WIDTH = ATT_GROUPS * ATT_KV_WIDTH
PEER_HEADS = 8
PEER_N_KEYS = 128
PEER_N_EXPERTS = PEER_N_KEYS * PEER_N_KEYS
PEER_HALF = 128
PEER_TOPK = 16
PEER_PICKS = PEER_HEADS * PEER_TOPK

LANES = 128
SUBLANES = 8
ROW_WORDS = D_MODEL // 2
ROW_SUBLANES = ROW_WORDS // LANES
VMEM_LIMIT = 56 * 1024 * 1024

IN_CHUNK = 512
_IN_LAYOUT = (
    (0, 0, "gelu"), (0, 1, "gelu"), (1, 0, "gelu"), (1, 1, "gelu"),
    (2, 0, None), (2, 1, None), (2, 2, None), (3, 0, None), (4, 0, None),
    (5, 0, "sigmoid"), (5, 1, "sigmoid"), (6, 0, "sigmoid"), (6, 1, "sigmoid"),
)
_IN_WIDTHS = (1024, 1024, 1536, 512, 512, 1024, 1024)


def _params(*sem):
    return pltpu.CompilerParams(dimension_semantics=sem, vmem_limit_bytes=VMEM_LIMIT)


def _gelu(y):
    return 0.5 * y * (1.0 + lax.erf(y * np.float32(1.0 / np.sqrt(2.0))))


def _rms(x, g):
    return x * lax.rsqrt(jnp.mean(x * x, axis=-1, keepdims=True) + EPS) * g


def _full(shape):
    return pl.BlockSpec(shape, lambda *_: (0,) * len(shape))


def _store_row_split(ref, x):
    tm = x.shape[0]
    for s in range(SUBLANES):
        ref[pl.ds(s, tm, stride=SUBLANES), :] = x[:, s * LANES:(s + 1) * LANES]


def _load_row_split(ref):
    tm = ref.shape[0] // SUBLANES
    return jnp.concatenate([ref[pl.ds(s, tm, stride=SUBLANES), :] for s in range(SUBLANES)], axis=1)


def _row_split_spec(tm):
    return pl.BlockSpec((tm * SUBLANES, LANES), lambda i: (i, 0))


def _in_proj_kernel(x_ref, g_ref, w_ref, *out_refs):
    h = _rms(x_ref[...], g_ref[...]).astype(BF16)
    for c, (slot, sub, epi) in enumerate(_IN_LAYOUT):
        y = jnp.dot(h, w_ref[:, c * IN_CHUNK:(c + 1) * IN_CHUNK], preferred_element_type=F32)
        if epi == "gelu":
            y = _gelu(y)
        elif epi == "sigmoid":
            y = jax.nn.sigmoid(y)
        out_refs[slot][:, sub * IN_CHUNK:(sub + 1) * IN_CHUNK] = y.astype(BF16)


def _in_proj(x2d, g, w_bf16, tm=512):
    n = x2d.shape[0]
    width = w_bf16.shape[1]
    return pl.pallas_call(
        _in_proj_kernel,
        grid=(n // tm,),
        in_specs=[pl.BlockSpec((tm, D_MODEL), lambda i: (i, 0)),
                  _full((1, D_MODEL)),
                  pl.BlockSpec((D_MODEL, width), lambda i: (0, 0), pipeline_mode=pl.Buffered(1))],
        out_specs=[pl.BlockSpec((tm, w), lambda i: (i, 0)) for w in _IN_WIDTHS],
        out_shape=[jax.ShapeDtypeStruct((n, w), BF16) for w in _IN_WIDTHS],
        compiler_params=_params("parallel"),
        name="in_proj",
    )(x2d, g, w_bf16)


def _sgu_kernel(u_ref, v_ref, ga_ref, ng_ref, ws_ref, bs_ref, wa_ref, out_ref):
    tm = u_ref.shape[0]
    vn = _rms(v_ref[...].astype(F32), ng_ref[...]).astype(BF16)
    chunks = []
    for c in range(tm // SGU_CHUNK):
        cols = []
        for g in range(SGU_GROUPS):
            blk = vn[c * SGU_CHUNK:(c + 1) * SGU_CHUNK, g * LANES:(g + 1) * LANES]
            cols.append(jnp.dot(ws_ref[g], blk, preferred_element_type=F32) + bs_ref[g])
        chunks.append(jnp.concatenate(cols, axis=1))
    mixed = jnp.concatenate(chunks, axis=0)
    ya = (u_ref[...].astype(F32) * mixed).astype(BF16)
    pa = jnp.dot(ya, wa_ref[...], preferred_element_type=F32) * ga_ref[...].astype(F32)
    out_ref[...] = pa.astype(BF16)


def _sgu(u, v, ga, ng, ws_bf16, bs_b, wa_bf16, tm=512):
    n = u.shape[0]
    row = pl.BlockSpec((tm, SGU_WIDTH), lambda i: (i, 0))
    return pl.pallas_call(
        _sgu_kernel,
        grid=(n // tm,),
        in_specs=[row, row, row, _full((1, SGU_WIDTH)),
                  _full((SGU_GROUPS, SGU_CHUNK, SGU_CHUNK)),
                  _full((SGU_GROUPS, SGU_CHUNK, LANES)),
                  _full((SGU_WIDTH, D_MODEL))],
        out_specs=pl.BlockSpec((tm, D_MODEL), lambda i: (i, 0)),
        out_shape=jax.ShapeDtypeStruct((n, D_MODEL), BF16),
        compiler_params=_params("parallel"),
        name="sgu",
    )(u, v, ga, ng, ws_bf16, bs_b, wa_bf16)


def _attn_kernel(q_ref, kp_ref, kc_ref, kn_ref, vp_ref, vc_ref, vn_ref, o_ref, lse_ref,
                 *, dilation, n_side, sub_len):
    tq = q_ref.shape[0]
    win = tq + 2 * n_side
    i = pl.program_id(2)
    k = jnp.concatenate([kp_ref[...], kc_ref[...], kn_ref[...]], axis=0)
    v = jnp.concatenate([vp_ref[...], vc_ref[...], vn_ref[...]], axis=0)
    row = lax.broadcasted_iota(I32, (tq, win), 0)
    col = lax.broadcasted_iota(I32, (tq, win), 1)
    rel = col - n_side - row
    pos = i * tq - n_side + col
    valid = (jnp.abs(rel) <= n_side) & (pos >= 0) & (pos < sub_len)
    negdist = -(jnp.abs(rel) * dilation).astype(F32)
    lane = lax.broadcasted_iota(I32, (tq, LANES), 1)
    low_half = lane < ATT_HEAD_DIM
    scale = np.float32(ATT_HEAD_DIM ** -0.5)
    for pair in range(ATT_HEADS // 2):
        sl = slice(pair * LANES, (pair + 1) * LANES)
        qp, kpair, vpair = q_ref[:, sl], k[:, sl], v[:, sl]
        outs, lses = [], []
        for j in range(2):
            slope = np.float32(2.0 ** (-8.0 * (pair * 2 + j + 1) / ATT_HEADS))
            qm = jnp.where(low_half if j == 0 else jnp.logical_not(low_half), qp, jnp.zeros_like(qp))
            s = lax.dot_general(qm, kpair, (((1,), (1,)), ((), ())), preferred_element_type=F32)
            s = jnp.where(valid, s * scale + slope * negdist, NEG_INF)
            m = jnp.max(s, axis=-1, keepdims=True)
            p = jnp.exp(s - m)
            l = jnp.sum(p, axis=-1, keepdims=True)
            o = jnp.dot(p.astype(BF16), vpair, preferred_element_type=F32)
            outs.append(o / l)
            lses.append(jnp.broadcast_to(m + jnp.log(l), (tq, LANES)))
        o_ref[:, sl] = jnp.where(low_half, outs[0], outs[1]).astype(o_ref.dtype)
        lse_ref[:, sl] = jnp.where(low_half, lses[0], lses[1])


def _attention_pattern(q3, k3, v3, group, dilation, n_side, tq=128):
    b, s, _ = q3.shape
    sub_len = s // dilation
    kvw = ATT_KV_WIDTH
    qv = q3.reshape(b, sub_len, dilation * ATT_Q_WIDTH)
    kv = k3.reshape(b, sub_len, dilation * kvw)
    vv = v3.reshape(b, sub_len, dilation * kvw)
    per = tq // n_side
    last = sub_len // n_side - 1
    cur = pl.BlockSpec((None, tq, kvw), lambda bi, r, i: (bi, i, r))
    prev = pl.BlockSpec((None, n_side, kvw), lambda bi, r, i: (bi, jnp.maximum(i * per - 1, 0), r))
    nxt = pl.BlockSpec((None, n_side, kvw), lambda bi, r, i: (bi, jnp.minimum((i + 1) * per, last), r))
    o, lse = pl.pallas_call(
        functools.partial(_attn_kernel, dilation=dilation, n_side=n_side, sub_len=sub_len),
        grid=(b, dilation, sub_len // tq),
        in_specs=[pl.BlockSpec((None, tq, kvw), lambda bi, r, i: (bi, i, r * ATT_GROUPS + group)),
                  prev, cur, nxt, prev, cur, nxt],
        out_specs=[cur, cur],
        out_shape=[jax.ShapeDtypeStruct((b, sub_len, dilation * kvw), BF16),
                   jax.ShapeDtypeStruct((b, sub_len, dilation * kvw), F32)],
        compiler_params=_params("parallel", "parallel", "parallel"),
        name=f"attn_d{dilation}",
    )(qv, kv, kv, kv, vv, vv, vv)
    return o.reshape(b * s, kvw), lse.reshape(b * s, kvw)


def _merge_kernel(x_ref, pa_ref, gb_ref, o1_ref, o2_ref, o3_ref, l1_ref, l2_ref, l3_ref,
                  wb_ref, wo_ref, g_ref, x1_ref, h2_ref):
    l1, l2, l3 = l1_ref[...], l2_ref[...], l3_ref[...]
    m = jnp.maximum(jnp.maximum(l1, l2), l3)
    w1, w2, w3 = jnp.exp(l1 - m), jnp.exp(l2 - m), jnp.exp(l3 - m)
    yb = (w1 * o1_ref[...].astype(F32) + w2 * o2_ref[...].astype(F32)
          + w3 * o3_ref[...].astype(F32)) / (w1 + w2 + w3)
    pb = jnp.dot(yb.astype(BF16), wb_ref[...], preferred_element_type=F32) * gb_ref[...].astype(F32)
    merged = (pa_ref[...].astype(F32) + pb).astype(BF16)
    x1 = x_ref[...] + jnp.dot(merged, wo_ref[...], preferred_element_type=F32)
    _store_row_split(x1_ref, x1)
    _store_row_split(h2_ref, _rms(x1, g_ref[...]))


def _merge(x2d, pa, gb, os_, ls_, wb_bf16, wo_bf16, g, tm=512):
    n = x2d.shape[0]
    wide = pl.BlockSpec((tm, D_MODEL), lambda i: (i, 0))
    half = pl.BlockSpec((tm, ATT_KV_WIDTH), lambda i: (i, 0))
    return pl.pallas_call(
        _merge_kernel,
        grid=(n // tm,),
        in_specs=[wide, wide, wide, half, half, half, half, half, half,
                  _full((ATT_KV_WIDTH, D_MODEL)), _full((D_MODEL, D_MODEL)), _full((1, D_MODEL))],
        out_specs=[_row_split_spec(tm), _row_split_spec(tm)],
        out_shape=[jax.ShapeDtypeStruct((n * SUBLANES, LANES), F32)] * 2,
        compiler_params=_params("parallel"),
        name="merge",
    )(x2d, pa, gb, *os_, *ls_, wb_bf16, wo_bf16, g)


def _topk_rows(s, k, payload=None):
    rows = lax.broadcasted_iota(I32, s.shape, 0).astype(F32)
    vals, ids = [], []
    for _ in range(k):
        m = jnp.max(s, axis=0, keepdims=True)
        am = jnp.min(jnp.where(s == m, rows, np.float32(s.shape[0])), axis=0, keepdims=True)
        hit = rows == am
        vals.append(m)
        if payload is None:
            ids.append(am)
        else:
            ids.append(jnp.max(jnp.where(hit, payload, -1.0), axis=0, keepdims=True))
        s = jnp.where(hit, -jnp.inf, s)
    return jnp.concatenate(vals, axis=0), jnp.concatenate(ids, axis=0)


def _pair_candidates(s0, s1, i0, i1):
    vals, ids = [], []
    for a in range(SUBLANES):
        n_b = PEER_TOPK // (a + 1)
        rows = PEER_TOPK if n_b > SUBLANES else SUBLANES
        v = s0[a:a + 1] + s1[0:rows]
        if n_b < rows:
            v = jnp.where(lax.broadcasted_iota(I32, v.shape, 0) < n_b, v, -jnp.inf)
        vals.append(v)
        ids.append(i0[a:a + 1] * PEER_N_KEYS + i1[0:rows])
    vals.append(s0[SUBLANES:PEER_TOPK] + s1[0:1])
    ids.append(i0[SUBLANES:PEER_TOPK] * PEER_N_KEYS + i1[0:1])
    return jnp.concatenate(vals, axis=0), jnp.concatenate(ids, axis=0)


def _split_bf16(a):
    hi = a.astype(BF16)
    return hi, (a - hi.astype(F32)).astype(BF16)


def _peer_topk_kernel(h_ref, wq_ref, skh_ref, skl_ref, idx_ref, gate_ref):
    q = jnp.dot(_load_row_split(h_ref).astype(BF16), wq_ref[...], preferred_element_type=F32)
    nt = (((1,), (1,)), ((), ()))
    idx_rows, gate_rows = [], []
    for hd in range(PEER_HEADS):
        top_s, top_i = [], []
        for p in range(2):
            hp = hd * 2 + p
            q_hi, q_lo = _split_bf16(q[:, hp * PEER_HALF:(hp + 1) * PEER_HALF])
            kh, kl = skh_ref[hp], skl_ref[hp]
            st = (lax.dot_general(kh, q_hi, nt, preferred_element_type=F32)
                  + lax.dot_general(kh, q_lo, nt, preferred_element_type=F32)
                  + lax.dot_general(kl, q_hi, nt, preferred_element_type=F32))
            ts, ti = _topk_rows(st, PEER_TOPK)
            top_s.append(ts)
            top_i.append(ti)
        cand, cidx = _pair_candidates(top_s[0], top_s[1], top_i[0], top_i[1])
        best_s, expert = _topk_rows(cand, PEER_TOPK, payload=cidx)
        e = jnp.exp(best_s - best_s[0:1])
        gate_rows.append(e / jnp.sum(e, axis=0, keepdims=True))
        idx_rows.append(expert * np.float32(ROW_SUBLANES))
    idx_ref[...] = jnp.concatenate(idx_rows, axis=0).T.astype(I32)
    gate_ref[...] = jnp.concatenate(gate_rows, axis=0).T


def _peer_topk(h2_split, wq_bf16, sk_hi, sk_lo, tm=256):
    n = h2_split.shape[0] // SUBLANES
    qw = wq_bf16.shape[1]
    out = pl.BlockSpec((tm, PEER_PICKS), lambda i: (i, 0))
    return pl.pallas_call(
        _peer_topk_kernel,
        grid=(n // tm,),
        in_specs=[_row_split_spec(tm),
                  _full((D_MODEL, qw)),
                  _full((2 * PEER_HEADS, PEER_N_KEYS, PEER_HALF)),
                  _full((2 * PEER_HEADS, PEER_N_KEYS, PEER_HALF))],
        out_specs=[out, out],
        out_shape=[jax.ShapeDtypeStruct((n, PEER_PICKS), I32), jax.ShapeDtypeStruct((n, PEER_PICKS), F32)],
        compiler_params=_params("parallel"),
        name="peer_topk",
    )(h2_split, wq_bf16, sk_hi, sk_lo)


def _pack_table(tab):
    lo16 = lax.bitcast_convert_type(tab[:, :ROW_WORDS].astype(BF16), jnp.uint16).astype(jnp.uint32)
    full = lax.bitcast_convert_type(tab[:, ROW_WORDS:], jnp.uint32)
    sign = full & jnp.uint32(0x80000000)
    mag = (full & jnp.uint32(0x7FFFFFFF)) + jnp.uint32(0x8000)
    hi16 = jnp.where(mag >= lo16, (mag - lo16) >> 16, jnp.uint32(0))
    word = sign | (hi16 << 16) | lo16
    return lax.bitcast_convert_type(word, I32).reshape(tab.shape[0] * ROW_SUBLANES, LANES)


def _gather_row(tab_ref, e4):
    w = tab_ref[pl.ds(pl.multiple_of(e4, ROW_SUBLANES), ROW_SUBLANES), :]
    return lax.bitcast_convert_type(w << 16, F32), lax.bitcast_convert_type(w, F32)


def _row_sums_all_lanes(rows, ones):
    r_hi, r_lo = _split_bf16(rows)
    return (jnp.dot(r_hi, ones, preferred_element_type=F32)
            + jnp.dot(r_lo, ones, preferred_element_type=F32))


TOKENS_PER_GROUP = SUBLANES


def _peer_u_kernel(idx_ref, h_ref, gate_ref, tab_ref, act_ref, slot_ref):
    tm = gate_ref.shape[0]
    slot_rows = PEER_PICKS * ROW_SUBLANES
    ones = jnp.ones((LANES, LANES), BF16)
    diag = (lax.broadcasted_iota(I32, (PEER_PICKS, LANES), 0)
            == lax.broadcasted_iota(I32, (PEER_PICKS, LANES), 1))

    def group(g, carry):
        def token(tt, carry2):
            t = g * TOKENS_PER_GROUP + tt
            hv = h_ref[pl.ds(pl.multiple_of(t * SUBLANES, SUBLANES), SUBLANES), :]
            h_lo, h_hi = hv[0:ROW_SUBLANES], hv[ROW_SUBLANES:SUBLANES]
            base = pl.multiple_of(tt * slot_rows, slot_rows)
            for k in range(PEER_PICKS):
                lo, hi = _gather_row(tab_ref, idx_ref[t, k])
                slot_ref[pl.ds(base + k * ROW_SUBLANES, ROW_SUBLANES), :] = lo * h_lo + hi * h_hi
            return carry2

        lax.fori_loop(0, TOKENS_PER_GROUP, token, 0)
        n_rows = TOKENS_PER_GROUP * PEER_PICKS
        r = slot_ref[pl.ds(0, n_rows, stride=ROW_SUBLANES), :]
        for s in range(1, ROW_SUBLANES):
            r = r + slot_ref[pl.ds(s, n_rows, stride=ROW_SUBLANES), :]
        rs = _row_sums_all_lanes(r, ones)
        rows = [jnp.sum(jnp.where(diag, rs[tt * PEER_PICKS:(tt + 1) * PEER_PICKS], 0.0), axis=0, keepdims=True)
                for tt in range(TOKENS_PER_GROUP)]
        a = jnp.concatenate(rows, axis=0)
        off = pl.multiple_of(g * TOKENS_PER_GROUP, TOKENS_PER_GROUP)
        act_ref[pl.ds(off, TOKENS_PER_GROUP), :] = _gelu(a) * gate_ref[pl.ds(off, TOKENS_PER_GROUP), :]
        return carry

    lax.fori_loop(0, tm // TOKENS_PER_GROUP, group, 0)


def _peer_u(idx, h2_split, gate, tab, tm=128):
    n = idx.shape[0]
    return pl.pallas_call(
        _peer_u_kernel,
        grid=(n // tm,),
        in_specs=[pl.BlockSpec((tm, PEER_PICKS), lambda i: (i, 0), memory_space=pltpu.SMEM),
                  _row_split_spec(tm),
                  pl.BlockSpec((tm, PEER_PICKS), lambda i: (i, 0)),
                  pl.BlockSpec(memory_space=pltpu.VMEM)],
        out_specs=pl.BlockSpec((tm, PEER_PICKS), lambda i: (i, 0)),
        out_shape=jax.ShapeDtypeStruct((n, PEER_PICKS), F32),
        scratch_shapes=[pltpu.VMEM((TOKENS_PER_GROUP * PEER_PICKS * ROW_SUBLANES, LANES), F32)],
        compiler_params=_params("arbitrary"),
        name="peer_u",
    )(idx, h2_split, gate, tab)


N_ACC = 4


def _peer_v_kernel(idx_ref, act_ref, x_ref, g_ref, tab_ref, out_ref, actb_ref, x2_ref, *, normalize):
    tm = idx_ref.shape[0]
    ones = jnp.ones((LANES, LANES), BF16)
    diag = (lax.broadcasted_iota(I32, (PEER_PICKS, LANES), 0)
            == lax.broadcasted_iota(I32, (PEER_PICKS, LANES), 1))

    def group(g, carry):
        off = pl.multiple_of(g * TOKENS_PER_GROUP, TOKENS_PER_GROUP)
        a8 = act_ref[pl.ds(off, TOKENS_PER_GROUP), :]
        spread = jnp.concatenate([jnp.where(diag, a8[tt:tt + 1, :], 0.0) for tt in range(TOKENS_PER_GROUP)], axis=0)
        actb_ref[...] = _row_sums_all_lanes(spread, ones)

        def token(tt, carry2):
            t = g * TOKENS_PER_GROUP + tt
            base = pl.multiple_of(tt * PEER_PICKS, PEER_PICKS)
            acc_lo = [jnp.zeros((ROW_SUBLANES, LANES), F32) for _ in range(N_ACC)]
            acc_hi = [jnp.zeros((ROW_SUBLANES, LANES), F32) for _ in range(N_ACC)]
            for k in range(PEER_PICKS):
                lo, hi = _gather_row(tab_ref, idx_ref[t, k])
                a = actb_ref[pl.ds(base + k, 1), :]
                acc_lo[k % N_ACC] = acc_lo[k % N_ACC] + a * lo
                acc_hi[k % N_ACC] = acc_hi[k % N_ACC] + a * hi
            lo = (acc_lo[0] + acc_lo[1]) + (acc_lo[2] + acc_lo[3])
            hi = (acc_hi[0] + acc_hi[1]) + (acc_hi[2] + acc_hi[3])
            r0 = pl.multiple_of(t * SUBLANES, SUBLANES)
            r1 = pl.multiple_of(t * SUBLANES + ROW_SUBLANES, ROW_SUBLANES)
            x2_ref[pl.ds(r0, ROW_SUBLANES), :] = x_ref[pl.ds(r0, ROW_SUBLANES), :] + lo
            x2_ref[pl.ds(r1, ROW_SUBLANES), :] = x_ref[pl.ds(r1, ROW_SUBLANES), :] + hi
            return carry2

        lax.fori_loop(0, TOKENS_PER_GROUP, token, 0)
        return carry

    lax.fori_loop(0, tm // TOKENS_PER_GROUP, group, 0)
    x2 = _load_row_split(x2_ref)
    out_ref[...] = _rms(x2, g_ref[...]) if normalize else x2


def _peer_v(idx, act, x1_split, tab, final_g, tm=128):
    n = idx.shape[0]
    normalize = final_g is not None
    g = final_g if normalize else jnp.ones((1, D_MODEL), F32)
    return pl.pallas_call(
        functools.partial(_peer_v_kernel, normalize=normalize),
        grid=(n // tm,),
        in_specs=[pl.BlockSpec((tm, PEER_PICKS), lambda i: (i, 0), memory_space=pltpu.SMEM),
                  pl.BlockSpec((tm, PEER_PICKS), lambda i: (i, 0)),
                  _row_split_spec(tm), _full((1, D_MODEL)), pl.BlockSpec(memory_space=pltpu.VMEM)],
        out_specs=pl.BlockSpec((tm, D_MODEL), lambda i: (i, 0)),
        out_shape=jax.ShapeDtypeStruct((n, D_MODEL), F32),
        scratch_shapes=[pltpu.VMEM((TOKENS_PER_GROUP * PEER_PICKS, LANES), F32),
                        pltpu.VMEM((tm * SUBLANES, LANES), F32)],
        compiler_params=_params("arbitrary"),
        name="peer_v",
    )(idx, act, x1_split, g, tab)


def kernel(x, norm_mix_g, w_in, sgu_norm_g, sgu_w, sgu_b, w_branch_a, w_branch_b, w_out,
           norm_ffn_g, peer_wq, peer_subkeys, peer_u, peer_v, norm_final_g):
    b, s, d = x.shape
    n = b * s
    x2 = x.reshape(n, d)
    for l in range(w_in.shape[0]):
        u, v, q, k, vv, ga, gb = _in_proj(x2, norm_mix_g[l].reshape(1, d), w_in[l].astype(BF16))
        bs_b = jnp.broadcast_to(sgu_b[l][:, :, None], (SGU_GROUPS, SGU_CHUNK, LANES))
        pa = _sgu(u, v, ga, sgu_norm_g[l].reshape(1, SGU_WIDTH), sgu_w[l].astype(BF16), bs_b,
                  w_branch_a[l].astype(BF16))
        q3, k3, v3 = (t.reshape(b, s, -1) for t in (q, k, vv))
        os_, ls_ = [], []
        for g, (window, dilation) in enumerate(ATT_PATTERNS):
            o, lse = _attention_pattern(q3, k3, v3, g, dilation, window // (2 * dilation))
            os_.append(o)
            ls_.append(lse)
        x1, h2 = _merge(x2, pa, gb, os_, ls_, w_branch_b[l].astype(BF16), w_out[l].astype(BF16),
                        norm_ffn_g[l].reshape(1, d))
        sk = peer_subkeys[l].reshape(2 * PEER_HEADS, PEER_N_KEYS, PEER_HALF)
        sk_hi = sk.astype(BF16)
        sk_lo = (sk - sk_hi.astype(F32)).astype(BF16)
        idx, gate = _peer_topk(h2, peer_wq[l].astype(BF16), sk_hi, sk_lo)
        act = _peer_u(idx, h2, gate, _pack_table(peer_u[l]))
        last = l == w_in.shape[0] - 1
        x2 = _peer_v(idx, act, x1, _pack_table(peer_v[l]), norm_final_g.reshape(1, d) if last else None)
    return x2.reshape(b, s, d)
```

```python
import functools

import numpy as np
import jax
import jax.numpy as jnp
from jax import lax
from jax.experimental import pallas as pl
from jax.experimental.pallas import tpu as pltpu

F32 = jnp.float32
BF16 = jnp.bfloat16
I32 = jnp.int32

D_MODEL = 1024
EPS = 1e-6
NEG_INF = -1e30
SGU_CHUNK = 128
SGU_GROUPS = 8
SGU_WIDTH = 1024
ATT_HEADS = 8
ATT_HEAD_DIM = 64
ATT_PATTERNS = ((128, 1), (512, 4), (2048, 16))
ATT_GROUPS = 3
ATT_KV_WIDTH = ATT_HEADS * ATT_HEAD_DIM
ATT_Q_WIDTH = ATT_GROUPS * ATT_KV_WIDTH
PEER_HEADS = 8
PEER_N_KEYS = 128
PEER_N_EXPERTS = PEER_N_KEYS * PEER_N_KEYS
PEER_HALF = 128
PEER_TOPK = 16
PEER_PICKS = PEER_HEADS * PEER_TOPK

LANES = 128
SUBLANES = 8
ROW_WORDS = D_MODEL // 2
ROW_SUBLANES = ROW_WORDS // LANES
VMEM_LIMIT = 56 * 1024 * 1024

IN_CHUNK = 512
_IN_LAYOUT = (
    (0, 0, "gelu"), (0, 1, "gelu"), (1, 0, "gelu"), (1, 1, "gelu"),
    (2, 0, None), (2, 1, None), (2, 2, None), (3, 0, None), (4, 0, None),
    (5, 0, "sigmoid"), (5, 1, "sigmoid"), (6, 0, "sigmoid"), (6, 1, "sigmoid"),
)
_IN_WIDTHS = (1024, 1024, 1536, 512, 512, 1024, 1024)


def _params(*sem):
    return pltpu.CompilerParams(dimension_semantics=sem, vmem_limit_bytes=VMEM_LIMIT)


def _gelu(y):
    return 0.5 * y * (1.0 + lax.erf(y * np.float32(1.0 / np.sqrt(2.0))))


def _rms(x, g):
    return x * lax.rsqrt(jnp.mean(x * x, axis=-1, keepdims=True) + EPS) * g


def _full(shape):
    return pl.BlockSpec(shape, lambda *_: (0,) * len(shape))


def _store_row_split(ref, x):
    tm = x.shape[0]
    for s in range(SUBLANES):
        ref[pl.ds(s, tm, stride=SUBLANES), :] = x[:, s * LANES:(s + 1) * LANES]


def _load_row_split(ref):
    tm = ref.shape[0] // SUBLANES
    return jnp.concatenate([ref[pl.ds(s, tm, stride=SUBLANES), :] for s in range(SUBLANES)], axis=1)


def _row_split_spec(tm):
    return pl.BlockSpec((tm * SUBLANES, LANES), lambda i: (i, 0))


def _in_proj_kernel(x_ref, g_ref, w_ref, *out_refs):
    h = _rms(x_ref[...], g_ref[...]).astype(BF16)
    for c, (slot, sub, epi) in enumerate(_IN_LAYOUT):
        y = jnp.dot(h, w_ref[:, c * IN_CHUNK:(c + 1) * IN_CHUNK], preferred_element_type=F32)
        if epi == "gelu":
            y = _gelu(y)
        elif epi == "sigmoid":
            y = jax.nn.sigmoid(y)
        out_refs[slot][:, sub * IN_CHUNK:(sub + 1) * IN_CHUNK] = y.astype(BF16)


def _in_proj(x2d, g, w_bf16, tm=512):
    n = x2d.shape[0]
    width = w_bf16.shape[1]
    return pl.pallas_call(
        _in_proj_kernel,
        grid=(n // tm,),
        in_specs=[pl.BlockSpec((tm, D_MODEL), lambda i: (i, 0)),
                  _full((1, D_MODEL)),
                  pl.BlockSpec((D_MODEL, width), lambda i: (0, 0), pipeline_mode=pl.Buffered(1))],
        out_specs=[pl.BlockSpec((tm, w), lambda i: (i, 0)) for w in _IN_WIDTHS],
        out_shape=[jax.ShapeDtypeStruct((n, w), BF16) for w in _IN_WIDTHS],
        compiler_params=_params("parallel"),
        name="in_proj",
    )(x2d, g, w_bf16)


def _sgu_kernel(u_ref, v_ref, ga_ref, ng_ref, ws_ref, bs_ref, wa_ref, out_ref):
    tm = u_ref.shape[0]
    vn = _rms(v_ref[...].astype(F32), ng_ref[...]).astype(BF16)
    chunks = []
    for c in range(tm // SGU_CHUNK):
        cols = []
        for g in range(SGU_GROUPS):
            blk = vn[c * SGU_CHUNK:(c + 1) * SGU_CHUNK, g * LANES:(g + 1) * LANES]
            cols.append(jnp.dot(ws_ref[g], blk, preferred_element_type=F32) + bs_ref[g])
        chunks.append(jnp.concatenate(cols, axis=1))
    mixed = jnp.concatenate(chunks, axis=0)
    ya = (u_ref[...].astype(F32) * mixed).astype(BF16)
    pa = jnp.dot(ya, wa_ref[...], preferred_element_type=F32) * ga_ref[...].astype(F32)
    out_ref[...] = pa.astype(BF16)


def _sgu(u, v, ga, ng, ws_bf16, bs_b, wa_bf16, tm=512):
    n = u.shape[0]
    row = pl.BlockSpec((tm, SGU_WIDTH), lambda i: (i, 0))
    return pl.pallas_call(
        _sgu_kernel,
        grid=(n // tm,),
        in_specs=[row, row, row, _full((1, SGU_WIDTH)),
                  _full((SGU_GROUPS, SGU_CHUNK, SGU_CHUNK)),
                  _full((SGU_GROUPS, SGU_CHUNK, LANES)),
                  _full((SGU_WIDTH, D_MODEL))],
        out_specs=pl.BlockSpec((tm, D_MODEL), lambda i: (i, 0)),
        out_shape=jax.ShapeDtypeStruct((n, D_MODEL), BF16),
        compiler_params=_params("parallel"),
        name="sgu",
    )(u, v, ga, ng, ws_bf16, bs_b, wa_bf16)


def _attn_kernel(q_ref, kp_ref, kc_ref, kn_ref, vp_ref, vc_ref, vn_ref, o_ref, lse_ref,
                 *, dilation, n_side, sub_len):
    tq = q_ref.shape[0]
    win = tq + 2 * n_side
    i = pl.program_id(2)
    k = jnp.concatenate([kp_ref[...], kc_ref[...], kn_ref[...]], axis=0)
    v = jnp.concatenate([vp_ref[...], vc_ref[...], vn_ref[...]], axis=0)
    row = lax.broadcasted_iota(I32, (tq, win), 0)
    col = lax.broadcasted_iota(I32, (tq, win), 1)
    rel = col - n_side - row
    pos = i * tq - n_side + col
    valid = (jnp.abs(rel) <= n_side) & (pos >= 0) & (pos < sub_len)
    negdist = -(jnp.abs(rel) * dilation).astype(F32)
    lane = lax.broadcasted_iota(I32, (tq, LANES), 1)
    low_half = lane < ATT_HEAD_DIM
    scale = np.float32(ATT_HEAD_DIM ** -0.5)
    for pair in range(ATT_HEADS // 2):
        sl = slice(pair * LANES, (pair + 1) * LANES)
        qp, kpair, vpair = q_ref[:, sl], k[:, sl], v[:, sl]
        outs, lses = [], []
        for j in range(2):
            slope = np.float32(2.0 ** (-8.0 * (pair * 2 + j + 1) / ATT_HEADS))
            qm = jnp.where(low_half if j == 0 else jnp.logical_not(low_half), qp, jnp.zeros_like(qp))
            s = lax.dot_general(qm, kpair, (((1,), (1,)), ((), ())), preferred_element_type=F32)
            s = jnp.where(valid, s * scale + slope * negdist, NEG_INF)
            m = jnp.max(s, axis=-1, keepdims=True)
            p = jnp.exp(s - m)
            l = jnp.sum(p, axis=-1, keepdims=True)
            o = jnp.dot(p.astype(BF16), vpair, preferred_element_type=F32)
            outs.append(o / l)
            lses.append(jnp.broadcast_to(m + jnp.log(l), (tq, LANES)))
        o_ref[:, sl] = jnp.where(low_half, outs[0], outs[1]).astype(o_ref.dtype)
        lse_ref[:, sl] = jnp.where(low_half, lses[0], lses[1])


def _attention_pattern(q3, k3, v3, group, dilation, n_side, tq=128):
    b, s, _ = q3.shape
    sub_len = s // dilation
    kvw = ATT_KV_WIDTH
    qv = q3.reshape(b, sub_len, dilation * ATT_Q_WIDTH)
    kv = k3.reshape(b, sub_len, dilation * kvw)
    vv = v3.reshape(b, sub_len, dilation * kvw)
    per = tq // n_side
    last = sub_len // n_side - 1
    cur = pl.BlockSpec((None, tq, kvw), lambda bi, r, i: (bi, i, r))
    prev = pl.BlockSpec((None, n_side, kvw), lambda bi, r, i: (bi, jnp.maximum(i * per - 1, 0), r))
    nxt = pl.BlockSpec((None, n_side, kvw), lambda bi, r, i: (bi, jnp.minimum((i + 1) * per, last), r))
    o, lse = pl.pallas_call(
        functools.partial(_attn_kernel, dilation=dilation, n_side=n_side, sub_len=sub_len),
        grid=(b, dilation, sub_len // tq),
        in_specs=[pl.BlockSpec((None, tq, kvw), lambda bi, r, i: (bi, i, r * ATT_GROUPS + group)),
                  prev, cur, nxt, prev, cur, nxt],
        out_specs=[cur, cur],
        out_shape=[jax.ShapeDtypeStruct((b, sub_len, dilation * kvw), BF16),
                   jax.ShapeDtypeStruct((b, sub_len, dilation * kvw), F32)],
        compiler_params=_params("parallel", "parallel", "parallel"),
        name=f"attn_d{dilation}",
    )(qv, kv, kv, kv, vv, vv, vv)
    return o.reshape(b * s, kvw), lse.reshape(b * s, kvw)


def _merge_kernel(x_ref, pa_ref, gb_ref, o1_ref, o2_ref, o3_ref, l1_ref, l2_ref, l3_ref,
                  wb_ref, wo_ref, g_ref, x1_ref, h2_ref):
    l1, l2, l3 = l1_ref[...], l2_ref[...], l3_ref[...]
    m = jnp.maximum(jnp.maximum(l1, l2), l3)
    w1, w2, w3 = jnp.exp(l1 - m), jnp.exp(l2 - m), jnp.exp(l3 - m)
    yb = (w1 * o1_ref[...].astype(F32) + w2 * o2_ref[...].astype(F32)
          + w3 * o3_ref[...].astype(F32)) / (w1 + w2 + w3)
    pb = jnp.dot(yb.astype(BF16), wb_ref[...], preferred_element_type=F32) * gb_ref[...].astype(F32)
    merged = (pa_ref[...].astype(F32) + pb).astype(BF16)
    x1 = x_ref[...] + jnp.dot(merged, wo_ref[...], preferred_element_type=F32)
    _store_row_split(x1_ref, x1)
    _store_row_split(h2_ref, _rms(x1, g_ref[...]))


def _merge(x2d, pa, gb, os_, ls_, wb_bf16, wo_bf16, g, tm=512):
    n = x2d.shape[0]
    wide = pl.BlockSpec((tm, D_MODEL), lambda i: (i, 0))
    half = pl.BlockSpec((tm, ATT_KV_WIDTH), lambda i: (i, 0))
    return pl.pallas_call(
        _merge_kernel,
        grid=(n // tm,),
        in_specs=[wide, wide, wide, half, half, half, half, half, half,
                  _full((ATT_KV_WIDTH, D_MODEL)), _full((D_MODEL, D_MODEL)), _full((1, D_MODEL))],
        out_specs=[_row_split_spec(tm), _row_split_spec(tm)],
        out_shape=[jax.ShapeDtypeStruct((n * SUBLANES, LANES), F32)] * 2,
        compiler_params=_params("parallel"),
        name="merge",
    )(x2d, pa, gb, *os_, *ls_, wb_bf16, wo_bf16, g)


def _topk_rows(s, k, payload=None):
    rows = lax.broadcasted_iota(I32, s.shape, 0).astype(F32)
    vals, ids = [], []
    for _ in range(k):
        m = jnp.max(s, axis=0, keepdims=True)
        am = jnp.min(jnp.where(s == m, rows, np.float32(s.shape[0])), axis=0, keepdims=True)
        hit = rows == am
        vals.append(m)
        if payload is None:
            ids.append(am)
        else:
            ids.append(jnp.max(jnp.where(hit, payload, -1.0), axis=0, keepdims=True))
        s = jnp.where(hit, -jnp.inf, s)
    return jnp.concatenate(vals, axis=0), jnp.concatenate(ids, axis=0)


def _pair_candidates(s0, s1, i0, i1):
    vals, ids = [], []
    for a in range(SUBLANES):
        n_b = PEER_TOPK // (a + 1)
        rows = PEER_TOPK if n_b > SUBLANES else SUBLANES
        v = s0[a:a + 1] + s1[0:rows]
        if n_b < rows:
            v = jnp.where(lax.broadcasted_iota(I32, v.shape, 0) < n_b, v, -jnp.inf)
        vals.append(v)
        ids.append(i0[a:a + 1] * PEER_N_KEYS + i1[0:rows])
    vals.append(s0[SUBLANES:PEER_TOPK] + s1[0:1])
    ids.append(i0[SUBLANES:PEER_TOPK] * PEER_N_KEYS + i1[0:1])
    return jnp.concatenate(vals, axis=0), jnp.concatenate(ids, axis=0)


def _split_bf16(a):
    hi = a.astype(BF16)
    return hi, (a - hi.astype(F32)).astype(BF16)


def _peer_topk_kernel(h_ref, wq_ref, skh_ref, skl_ref, idx_ref, gate_ref):
    q = jnp.dot(_load_row_split(h_ref).astype(BF16), wq_ref[...], preferred_element_type=F32)
    nt = (((1,), (1,)), ((), ()))
    idx_rows, gate_rows = [], []
    for hd in range(PEER_HEADS):
        top_s, top_i = [], []
        for p in range(2):
            hp = hd * 2 + p
            q_hi, q_lo = _split_bf16(q[:, hp * PEER_HALF:(hp + 1) * PEER_HALF])
            kh, kl = skh_ref[hp], skl_ref[hp]
            st = (lax.dot_general(kh, q_hi, nt, preferred_element_type=F32)
                  + lax.dot_general(kh, q_lo, nt, preferred_element_type=F32)
                  + lax.dot_general(kl, q_hi, nt, preferred_element_type=F32))
            ts, ti = _topk_rows(st, PEER_TOPK)
            top_s.append(ts)
            top_i.append(ti)
        cand, cidx = _pair_candidates(top_s[0], top_s[1], top_i[0], top_i[1])
        best_s, expert = _topk_rows(cand, PEER_TOPK, payload=cidx)
        e = jnp.exp(best_s - best_s[0:1])
        gate_rows.append(e / jnp.sum(e, axis=0, keepdims=True))
        idx_rows.append(expert * np.float32(ROW_SUBLANES))
    idx_ref[...] = jnp.concatenate(idx_rows, axis=0).T.astype(I32)
    gate_ref[...] = jnp.concatenate(gate_rows, axis=0).T


def _peer_topk(h2_split, wq_bf16, sk_hi, sk_lo, tm=256):
    n = h2_split.shape[0] // SUBLANES
    qw = wq_bf16.shape[1]
    out = pl.BlockSpec((tm, PEER_PICKS), lambda i: (i, 0))
    return pl.pallas_call(
        _peer_topk_kernel,
        grid=(n // tm,),
        in_specs=[_row_split_spec(tm),
                  _full((D_MODEL, qw)),
                  _full((2 * PEER_HEADS, PEER_N_KEYS, PEER_HALF)),
                  _full((2 * PEER_HEADS, PEER_N_KEYS, PEER_HALF))],
        out_specs=[out, out],
        out_shape=[jax.ShapeDtypeStruct((n, PEER_PICKS), I32), jax.ShapeDtypeStruct((n, PEER_PICKS), F32)],
        compiler_params=_params("parallel"),
        name="peer_topk",
    )(h2_split, wq_bf16, sk_hi, sk_lo)


def _pack_table(tab):
    lo16 = lax.bitcast_convert_type(tab[:, :ROW_WORDS].astype(BF16), jnp.uint16).astype(jnp.uint32)
    full = lax.bitcast_convert_type(tab[:, ROW_WORDS:], jnp.uint32)
    sign = full & jnp.uint32(0x80000000)
    mag = (full & jnp.uint32(0x7FFFFFFF)) + jnp.uint32(0x8000)
    hi16 = jnp.where(mag >= lo16, (mag - lo16) >> 16, jnp.uint32(0))
    word = sign | (hi16 << 16) | lo16
    return lax.bitcast_convert_type(word, I32).reshape(tab.shape[0] * ROW_SUBLANES, LANES)


def _gather_row(tab_ref, e4):
    w = tab_ref[pl.ds(pl.multiple_of(e4, ROW_SUBLANES), ROW_SUBLANES), :]
    return lax.bitcast_convert_type(w << 16, F32), lax.bitcast_convert_type(w, F32)


def _lane_sums(rows):
    return jnp.dot(rows.astype(BF16), jnp.ones((LANES, LANES), BF16), preferred_element_type=F32)


def _pick_diag():
    return (lax.broadcasted_iota(I32, (PEER_PICKS, LANES), 0)
            == lax.broadcasted_iota(I32, (PEER_PICKS, LANES), 1))


TOKENS_PER_GROUP = 32


def _peer_u_kernel(idx_ref, h_ref, gate_ref, tab_ref, act_ref, slot_ref):
    tm = gate_ref.shape[0]
    slot_rows = PEER_PICKS * ROW_SUBLANES
    diag = _pick_diag()

    def group(g, carry):
        def token(tt, carry2):
            t = g * TOKENS_PER_GROUP + tt
            hv = h_ref[pl.ds(pl.multiple_of(t * SUBLANES, SUBLANES), SUBLANES), :]
            h_lo, h_hi = hv[0:ROW_SUBLANES], hv[ROW_SUBLANES:SUBLANES]
            base = pl.multiple_of(tt * slot_rows, slot_rows)
            for k in range(PEER_PICKS):
                lo, hi = _gather_row(tab_ref, idx_ref[t, k])
                slot_ref[pl.ds(base + k * ROW_SUBLANES, ROW_SUBLANES), :] = lo * h_lo + hi * h_hi
            return carry2

        lax.fori_loop(0, TOKENS_PER_GROUP, token, 0)
        rows = []
        for tt in range(TOKENS_PER_GROUP):
            r = slot_ref[pl.ds(tt * slot_rows, PEER_PICKS, stride=ROW_SUBLANES), :]
            for s in range(1, ROW_SUBLANES):
                r = r + slot_ref[pl.ds(tt * slot_rows + s, PEER_PICKS, stride=ROW_SUBLANES), :]
            rows.append(jnp.sum(jnp.where(diag, _lane_sums(r), 0.0), axis=0, keepdims=True))
        off = pl.multiple_of(g * TOKENS_PER_GROUP, TOKENS_PER_GROUP)
        a = jnp.concatenate(rows, axis=0)
        act_ref[pl.ds(off, TOKENS_PER_GROUP), :] = _gelu(a) * gate_ref[pl.ds(off, TOKENS_PER_GROUP), :]
        return carry

    lax.fori_loop(0, tm // TOKENS_PER_GROUP, group, 0)


def _peer_u(idx, h2_split, gate, tab, tm=128):
    n = idx.shape[0]
    return pl.pallas_call(
        _peer_u_kernel,
        grid=(n // tm,),
        in_specs=[pl.BlockSpec((tm, PEER_PICKS), lambda i: (i, 0), memory_space=pltpu.SMEM),
                  _row_split_spec(tm),
                  pl.BlockSpec((tm, PEER_PICKS), lambda i: (i, 0)),
                  pl.BlockSpec(memory_space=pltpu.VMEM)],
        out_specs=pl.BlockSpec((tm, PEER_PICKS), lambda i: (i, 0)),
        out_shape=jax.ShapeDtypeStruct((n, PEER_PICKS), F32),
        scratch_shapes=[pltpu.VMEM((TOKENS_PER_GROUP * PEER_PICKS * ROW_SUBLANES, LANES), F32)],
        compiler_params=_params("arbitrary"),
        name="peer_u",
    )(idx, h2_split, gate, tab)


N_ACC = 4


def _peer_v_kernel(idx_ref, act_ref, x_ref, g_ref, tab_ref, out_ref, actb_ref, x2_ref, *, normalize):
    tm = idx_ref.shape[0]
    diag = _pick_diag()

    def group(g, carry):
        off = pl.multiple_of(g * TOKENS_PER_GROUP, TOKENS_PER_GROUP)
        for tt in range(TOKENS_PER_GROUP):
            spread = jnp.where(diag, act_ref[pl.ds(off + tt, 1), :], 0.0)
            actb_ref[tt * PEER_PICKS:(tt + 1) * PEER_PICKS, :] = _lane_sums(spread)

        def token(tt, carry2):
            t = g * TOKENS_PER_GROUP + tt
            base = pl.multiple_of(tt * PEER_PICKS, PEER_PICKS)
            acc_lo = [jnp.zeros((ROW_SUBLANES, LANES), F32) for _ in range(N_ACC)]
            acc_hi = [jnp.zeros((ROW_SUBLANES, LANES), F32) for _ in range(N_ACC)]
            for k in range(PEER_PICKS):
                lo, hi = _gather_row(tab_ref, idx_ref[t, k])
                a = actb_ref[pl.ds(base + k, 1), :]
                acc_lo[k % N_ACC] = acc_lo[k % N_ACC] + a * lo
                acc_hi[k % N_ACC] = acc_hi[k % N_ACC] + a * hi
            lo = (acc_lo[0] + acc_lo[1]) + (acc_lo[2] + acc_lo[3])
            hi = (acc_hi[0] + acc_hi[1]) + (acc_hi[2] + acc_hi[3])
            r0 = pl.multiple_of(t * SUBLANES, SUBLANES)
            r1 = pl.multiple_of(t * SUBLANES + ROW_SUBLANES, ROW_SUBLANES)
            x2_ref[pl.ds(r0, ROW_SUBLANES), :] = x_ref[pl.ds(r0, ROW_SUBLANES), :] + lo
            x2_ref[pl.ds(r1, ROW_SUBLANES), :] = x_ref[pl.ds(r1, ROW_SUBLANES), :] + hi
            return carry2

        lax.fori_loop(0, TOKENS_PER_GROUP, token, 0)
        return carry

    lax.fori_loop(0, tm // TOKENS_PER_GROUP, group, 0)
    x2 = _load_row_split(x2_ref)
    out_ref[...] = _rms(x2, g_ref[...]) if normalize else x2


def _peer_v(idx, act, x1_split, tab, final_g, tm=128):
    n = idx.shape[0]
    normalize = final_g is not None
    g = final_g if normalize else jnp.ones((1, D_MODEL), F32)
    return pl.pallas_call(
        functools.partial(_peer_v_kernel, normalize=normalize),
        grid=(n // tm,),
        in_specs=[pl.BlockSpec((tm, PEER_PICKS), lambda i: (i, 0), memory_space=pltpu.SMEM),
                  pl.BlockSpec((tm, PEER_PICKS), lambda i: (i, 0)),
                  _row_split_spec(tm), _full((1, D_MODEL)), pl.BlockSpec(memory_space=pltpu.VMEM)],
        out_specs=pl.BlockSpec((tm, D_MODEL), lambda i: (i, 0)),
        out_shape=jax.ShapeDtypeStruct((n, D_MODEL), F32),
        scratch_shapes=[pltpu.VMEM((TOKENS_PER_GROUP * PEER_PICKS, LANES), F32),
                        pltpu.VMEM((tm * SUBLANES, LANES), F32)],
        compiler_params=_params("arbitrary"),
        name="peer_v",
    )(idx, act, x1_split, g, tab)


def kernel(x, norm_mix_g, w_in, sgu_norm_g, sgu_w, sgu_b, w_branch_a, w_branch_b, w_out,
           norm_ffn_g, peer_wq, peer_subkeys, peer_u, peer_v, norm_final_g):
    b, s, d = x.shape
    n = b * s
    x2 = x.reshape(n, d)
    for l in range(w_in.shape[0]):
        u, v, q, k, vv, ga, gb = _in_proj(x2, norm_mix_g[l].reshape(1, d), w_in[l].astype(BF16))
        bs_b = jnp.broadcast_to(sgu_b[l][:, :, None], (SGU_GROUPS, SGU_CHUNK, LANES))
        pa = _sgu(u, v, ga, sgu_norm_g[l].reshape(1, SGU_WIDTH), sgu_w[l].astype(BF16), bs_b,
                  w_branch_a[l].astype(BF16))
        q3, k3, v3 = (t.reshape(b, s, -1) for t in (q, k, vv))
        os_, ls_ = [], []
        for g, (window, dilation) in enumerate(ATT_PATTERNS):
            o, lse = _attention_pattern(q3, k3, v3, g, dilation, window // (2 * dilation))
            os_.append(o)
            ls_.append(lse)
        x1, h2 = _merge(x2, pa, gb, os_, ls_, w_branch_b[l].astype(BF16), w_out[l].astype(BF16),
                        norm_ffn_g[l].reshape(1, d))
        sk = peer_subkeys[l].reshape(2 * PEER_HEADS, PEER_N_KEYS, PEER_HALF)
        sk_hi = sk.astype(BF16)
        sk_lo = (sk - sk_hi.astype(F32)).astype(BF16)
        idx, gate = _peer_topk(h2, peer_wq[l].astype(BF16), sk_hi, sk_lo)
        act = _peer_u(idx, h2, gate, _pack_table(peer_u[l]))
        last = l == w_in.shape[0] - 1
        x2 = _peer_v(idx, act, x1, _pack_table(peer_v[l]), norm_final_g.reshape(1, d) if last else None)
    return x2.reshape(b, s, d)
```

```python
import functools

import numpy as np
import jax
import jax.numpy as jnp
from jax import lax
from jax.experimental import pallas as pl
from jax.experimental.pallas import tpu as pltpu

F32 = jnp.float32
BF16 = jnp.bfloat16
I32 = jnp.int32

D_MODEL = 1024
EPS = 1e-6
NEG_INF = -1e30
SGU_CHUNK = 128
SGU_GROUPS = 8
SGU_WIDTH = 1024
ATT_HEADS = 8
ATT_HEAD_DIM = 64
ATT_PATTERNS = ((128, 1), (512, 4), (2048, 16))
ATT_GROUPS = 3
ATT_KV_WIDTH = ATT_HEADS * ATT_HEAD_DIM
ATT_Q_WIDTH = ATT_GROUPS * ATT_KV_WIDTH
PEER_HEADS = 8
PEER_N_KEYS = 128
PEER_N_EXPERTS = PEER_N_KEYS * PEER_N_KEYS
PEER_HALF = 128
PEER_TOPK = 16
PEER_PICKS = PEER_HEADS * PEER_TOPK

LANES = 128
SUBLANES = 8
ROW_WORDS = D_MODEL // 2
ROW_SUBLANES = ROW_WORDS // LANES
VMEM_LIMIT = 56 * 1024 * 1024

IN_CHUNK = 512
ATT_DILATIONS = tuple(d for _, d in ATT_PATTERNS)
_N_PLAIN = 4
_IN_LAYOUT = (
    ("gelu", (0, 0), ()), ("gelu", (0, 1), ()), ("gelu", (1, 0), ()), ("gelu", (1, 1), ()),
    *((None, None, ((_N_PLAIN + 3 * g, d),)) for g, d in enumerate(ATT_DILATIONS)),
    (None, None, tuple((_N_PLAIN + 3 * g + 1, d) for g, d in enumerate(ATT_DILATIONS))),
    (None, None, tuple((_N_PLAIN + 3 * g + 2, d) for g, d in enumerate(ATT_DILATIONS))),
    ("sigmoid", (2, 0), ()), ("sigmoid", (2, 1), ()), ("sigmoid", (3, 0), ()), ("sigmoid", (3, 1), ()),
)


def _params(*sem):
    return pltpu.CompilerParams(dimension_semantics=sem, vmem_limit_bytes=VMEM_LIMIT)


def _gelu(y):
    return 0.5 * y * (1.0 + lax.erf(y * np.float32(1.0 / np.sqrt(2.0))))


def _rms(x, g):
    return x * lax.rsqrt(jnp.mean(x * x, axis=-1, keepdims=True) + EPS) * g


def _full(shape):
    return pl.BlockSpec(shape, lambda *_: (0,) * len(shape))


def _store_row_split(ref, x):
    tm = x.shape[0]
    for s in range(SUBLANES):
        ref[pl.ds(s, tm, stride=SUBLANES), :] = x[:, s * LANES:(s + 1) * LANES]


def _load_row_split(ref):
    tm = ref.shape[0] // SUBLANES
    return jnp.concatenate([ref[pl.ds(s, tm, stride=SUBLANES), :] for s in range(SUBLANES)], axis=1)


def _row_split_spec(tm):
    return pl.BlockSpec((tm * SUBLANES, LANES), lambda i: (i, 0))


def _in_proj_kernel(x_ref, g_ref, w_ref, *refs):
    out_refs, ybuf_ref = refs[:-1], refs[-1]
    h = _rms(x_ref[...], g_ref[...]).astype(BF16)
    for c, (epi, plain, views) in enumerate(_IN_LAYOUT):
        y = jnp.dot(h, w_ref[:, c * IN_CHUNK:(c + 1) * IN_CHUNK], preferred_element_type=F32)
        if epi == "gelu":
            y = _gelu(y)
        elif epi == "sigmoid":
            y = jax.nn.sigmoid(y)
        if plain is not None:
            slot, sub = plain
            out_refs[slot][:, sub * IN_CHUNK:(sub + 1) * IN_CHUNK] = y.astype(BF16)
        if any(d > 1 for _, d in views):
            for j in range(IN_CHUNK // LANES):
                ybuf_ref[j] = y[:, j * LANES:(j + 1) * LANES]
        for slot, d in views:
            ref = out_refs[slot]
            if d == 1:
                ref[...] = y.astype(BF16)
                continue
            for r in range(d):
                for j in range(IN_CHUNK // LANES):
                    c0 = r * IN_CHUNK + j * LANES
                    ref[:, c0:c0 + LANES] = ybuf_ref[j, pl.ds(r, ref.shape[0], stride=d), :].astype(BF16)


def _in_proj(x2d, g, w_bf16, tm=512):
    n = x2d.shape[0]
    width = w_bf16.shape[1]
    shapes = [(n, SGU_WIDTH), (n, SGU_WIDTH), (n, D_MODEL), (n, D_MODEL)]
    blocks = [(tm, SGU_WIDTH), (tm, SGU_WIDTH), (tm, D_MODEL), (tm, D_MODEL)]
    for d in ATT_DILATIONS:
        shapes += [(n // d, d * IN_CHUNK)] * 3
        blocks += [(tm // d, d * IN_CHUNK)] * 3
    return pl.pallas_call(
        _in_proj_kernel,
        grid=(n // tm,),
        in_specs=[pl.BlockSpec((tm, D_MODEL), lambda i: (i, 0)),
                  _full((1, D_MODEL)),
                  pl.BlockSpec((D_MODEL, width), lambda i: (0, 0), pipeline_mode=pl.Buffered(1))],
        out_specs=[pl.BlockSpec(blk, lambda i: (i, 0)) for blk in blocks],
        out_shape=[jax.ShapeDtypeStruct(shp, BF16) for shp in shapes],
        scratch_shapes=[pltpu.VMEM((IN_CHUNK // LANES, tm, LANES), F32)],
        compiler_params=_params("parallel"),
        name="in_proj",
    )(x2d, g, w_bf16)


def _sgu_kernel(u_ref, v_ref, ga_ref, ng_ref, ws_ref, bs_ref, wa_ref, out_ref):
    tm = u_ref.shape[0]
    vn = _rms(v_ref[...].astype(F32), ng_ref[...]).astype(BF16)
    chunks = []
    for c in range(tm // SGU_CHUNK):
        cols = []
        for g in range(SGU_GROUPS):
            blk = vn[c * SGU_CHUNK:(c + 1) * SGU_CHUNK, g * LANES:(g + 1) * LANES]
            cols.append(jnp.dot(ws_ref[g], blk, preferred_element_type=F32) + bs_ref[g])
        chunks.append(jnp.concatenate(cols, axis=1))
    mixed = jnp.concatenate(chunks, axis=0)
    ya = (u_ref[...].astype(F32) * mixed).astype(BF16)
    pa = jnp.dot(ya, wa_ref[...], preferred_element_type=F32) * ga_ref[...].astype(F32)
    out_ref[...] = pa.astype(BF16)


def _sgu(u, v, ga, ng, ws_bf16, bs_b, wa_bf16, tm=512):
    n = u.shape[0]
    row = pl.BlockSpec((tm, SGU_WIDTH), lambda i: (i, 0))
    return pl.pallas_call(
        _sgu_kernel,
        grid=(n // tm,),
        in_specs=[row, row, row, _full((1, SGU_WIDTH)),
                  _full((SGU_GROUPS, SGU_CHUNK, SGU_CHUNK)),
                  _full((SGU_GROUPS, SGU_CHUNK, LANES)),
                  _full((SGU_WIDTH, D_MODEL))],
        out_specs=pl.BlockSpec((tm, D_MODEL), lambda i: (i, 0)),
        out_shape=jax.ShapeDtypeStruct((n, D_MODEL), BF16),
        compiler_params=_params("parallel"),
        name="sgu",
    )(u, v, ga, ng, ws_bf16, bs_b, wa_bf16)


def _attn_kernel(q_ref, kp_ref, kc_ref, kn_ref, vp_ref, vc_ref, vn_ref, o_ref, lse_ref,
                 *, dilation, n_side, sub_len):
    tq = q_ref.shape[0]
    win = tq + 2 * n_side
    i = pl.program_id(2)
    k = jnp.concatenate([kp_ref[...], kc_ref[...], kn_ref[...]], axis=0)
    v = jnp.concatenate([vp_ref[...], vc_ref[...], vn_ref[...]], axis=0)
    row = lax.broadcasted_iota(I32, (tq, win), 0)
    col = lax.broadcasted_iota(I32, (tq, win), 1)
    rel = col - n_side - row
    pos = i * tq - n_side + col
    valid = (jnp.abs(rel) <= n_side) & (pos >= 0) & (pos < sub_len)
    negdist = -(jnp.abs(rel) * dilation).astype(F32)
    lane = lax.broadcasted_iota(I32, (tq, LANES), 1)
    low_half = lane < ATT_HEAD_DIM
    scale = np.float32(ATT_HEAD_DIM ** -0.5)
    for pair in range(ATT_HEADS // 2):
        sl = slice(pair * LANES, (pair + 1) * LANES)
        qp, kpair, vpair = q_ref[:, sl], k[:, sl], v[:, sl]
        outs, lses = [], []
        for j in range(2):
            slope = np.float32(2.0 ** (-8.0 * (pair * 2 + j + 1) / ATT_HEADS))
            qm = jnp.where(low_half if j == 0 else jnp.logical_not(low_half), qp, jnp.zeros_like(qp))
            s = lax.dot_general(qm, kpair, (((1,), (1,)), ((), ())), preferred_element_type=F32)
            s = jnp.where(valid, s * scale + slope * negdist, NEG_INF)
            m = jnp.max(s, axis=-1, keepdims=True)
            p = jnp.exp(s - m)
            l = jnp.sum(p, axis=-1, keepdims=True)
            o = jnp.dot(p.astype(BF16), vpair, preferred_element_type=F32)
            outs.append(o / l)
            lses.append(jnp.broadcast_to(m + jnp.log(l), (tq, LANES)))
        o_ref[:, sl] = jnp.where(low_half, outs[0], outs[1]).astype(o_ref.dtype)
        lse_ref[:, sl] = jnp.where(low_half, lses[0], lses[1])


def _attention_pattern(qv, kv, vv, batch, dilation, n_side, tq=128):
    sub_len = qv.shape[0] // batch
    kvw = ATT_KV_WIDTH
    b = batch
    qv, kv, vv = (t.reshape(b, sub_len, dilation * kvw) for t in (qv, kv, vv))
    per = tq // n_side
    last = sub_len // n_side - 1
    cur = pl.BlockSpec((None, tq, kvw), lambda bi, r, i: (bi, i, r))
    prev = pl.BlockSpec((None, n_side, kvw), lambda bi, r, i: (bi, jnp.maximum(i * per - 1, 0), r))
    nxt = pl.BlockSpec((None, n_side, kvw), lambda bi, r, i: (bi, jnp.minimum((i + 1) * per, last), r))
    o, lse = pl.pallas_call(
        functools.partial(_attn_kernel, dilation=dilation, n_side=n_side, sub_len=sub_len),
        grid=(b, dilation, sub_len // tq),
        in_specs=[cur, prev, cur, nxt, prev, cur, nxt],
        out_specs=[cur, cur],
        out_shape=[jax.ShapeDtypeStruct((b, sub_len, dilation * kvw), BF16),
                   jax.ShapeDtypeStruct((b, sub_len, dilation * kvw), F32)],
        compiler_params=_params("parallel", "parallel", "parallel"),
        name=f"attn_d{dilation}",
    )(qv, kv, kv, kv, vv, vv, vv)
    return o.reshape(b * sub_len, dilation * kvw), lse.reshape(b * sub_len, dilation * kvw)


def _from_residue_view(blk_ref, tmp_ref, d):
    if d == 1:
        return blk_ref[...].astype(F32)
    n_tiles = IN_CHUNK // LANES
    for r in range(d):
        for j in range(n_tiles):
            c0 = r * IN_CHUNK + j * LANES
            tmp_ref[j, pl.ds(r, blk_ref.shape[0], stride=d), :] = blk_ref[:, c0:c0 + LANES].astype(F32)
    return jnp.concatenate([tmp_ref[j] for j in range(n_tiles)], axis=1)


def _merge_kernel(x_ref, pa_ref, gb_ref, o1_ref, o2_ref, o3_ref, l1_ref, l2_ref, l3_ref,
                  wb_ref, wo_ref, g_ref, x1_ref, h2_ref, *tmp_refs):
    tmp = iter(tmp_refs)
    os_, ls_ = [], []
    for o_ref, l_ref, d in zip((o1_ref, o2_ref, o3_ref), (l1_ref, l2_ref, l3_ref), ATT_DILATIONS):
        os_.append(_from_residue_view(o_ref, next(tmp) if d > 1 else None, d))
        ls_.append(_from_residue_view(l_ref, next(tmp) if d > 1 else None, d))
    l1, l2, l3 = ls_
    m = jnp.maximum(jnp.maximum(l1, l2), l3)
    w1, w2, w3 = jnp.exp(l1 - m), jnp.exp(l2 - m), jnp.exp(l3 - m)
    yb = (w1 * os_[0] + w2 * os_[1] + w3 * os_[2]) / (w1 + w2 + w3)
    pb = jnp.dot(yb.astype(BF16), wb_ref[...], preferred_element_type=F32) * gb_ref[...].astype(F32)
    merged = (pa_ref[...].astype(F32) + pb).astype(BF16)
    x1 = x_ref[...] + jnp.dot(merged, wo_ref[...], preferred_element_type=F32)
    _store_row_split(x1_ref, x1)
    _store_row_split(h2_ref, _rms(x1, g_ref[...]))


def _merge(x2d, pa, gb, os_, ls_, wb_bf16, wo_bf16, g, tm=512):
    n = x2d.shape[0]
    wide = pl.BlockSpec((tm, D_MODEL), lambda i: (i, 0))
    views = [pl.BlockSpec((tm // d, d * ATT_KV_WIDTH), lambda i: (i, 0)) for d in ATT_DILATIONS]
    n_tmp = 2 * sum(d > 1 for d in ATT_DILATIONS)
    return pl.pallas_call(
        _merge_kernel,
        grid=(n // tm,),
        in_specs=[wide, wide, wide, *views, *views,
                  _full((ATT_KV_WIDTH, D_MODEL)), _full((D_MODEL, D_MODEL)), _full((1, D_MODEL))],
        out_specs=[_row_split_spec(tm), _row_split_spec(tm)],
        out_shape=[jax.ShapeDtypeStruct((n * SUBLANES, LANES), F32)] * 2,
        scratch_shapes=[pltpu.VMEM((ATT_KV_WIDTH // LANES, tm, LANES), F32)] * n_tmp,
        compiler_params=_params("parallel"),
        name="merge",
    )(x2d, pa, gb, *os_, *ls_, wb_bf16, wo_bf16, g)


def _topk_rows(s, k, payload=None):
    rows = lax.broadcasted_iota(I32, s.shape, 0).astype(F32)
    vals, ids = [], []
    for _ in range(k):
        m = jnp.max(s, axis=0, keepdims=True)
        am = jnp.min(jnp.where(s == m, rows, np.float32(s.shape[0])), axis=0, keepdims=True)
        hit = rows == am
        vals.append(m)
        if payload is None:
            ids.append(am)
        else:
            ids.append(jnp.max(jnp.where(hit, payload, -1.0), axis=0, keepdims=True))
        s = jnp.where(hit, -jnp.inf, s)
    return jnp.concatenate(vals, axis=0), jnp.concatenate(ids, axis=0)


def _pair_candidates(s0, s1, i0, i1):
    vals, ids = [], []
    for a in range(SUBLANES):
        n_b = PEER_TOPK // (a + 1)
        rows = PEER_TOPK if n_b > SUBLANES else SUBLANES
        v = s0[a:a + 1] + s1[0:rows]
        if n_b < rows:
            v = jnp.where(lax.broadcasted_iota(I32, v.shape, 0) < n_b, v, -jnp.inf)
        vals.append(v)
        ids.append(i0[a:a + 1] * PEER_N_KEYS + i1[0:rows])
    vals.append(s0[SUBLANES:PEER_TOPK] + s1[0:1])
    ids.append(i0[SUBLANES:PEER_TOPK] * PEER_N_KEYS + i1[0:1])
    return jnp.concatenate(vals, axis=0), jnp.concatenate(ids, axis=0)


def _split_bf16(a):
    hi = a.astype(BF16)
    return hi, (a - hi.astype(F32)).astype(BF16)


def _peer_topk_kernel(h_ref, wq_ref, skh_ref, skl_ref, idx_ref, gate_ref):
    q = jnp.dot(_load_row_split(h_ref).astype(BF16), wq_ref[...], preferred_element_type=F32)
    nt = (((1,), (1,)), ((), ()))
    idx_rows, gate_rows = [], []
    for hd in range(PEER_HEADS):
        top_s, top_i = [], []
        for p in range(2):
            hp = hd * 2 + p
            q_hi, q_lo = _split_bf16(q[:, hp * PEER_HALF:(hp + 1) * PEER_HALF])
            kh, kl = skh_ref[hp], skl_ref[hp]
            st = (lax.dot_general(kh, q_hi, nt, preferred_element_type=F32)
                  + lax.dot_general(kh, q_lo, nt, preferred_element_type=F32)
                  + lax.dot_general(kl, q_hi, nt, preferred_element_type=F32))
            ts, ti = _topk_rows(st, PEER_TOPK)
            top_s.append(ts)
            top_i.append(ti)
        cand, cidx = _pair_candidates(top_s[0], top_s[1], top_i[0], top_i[1])
        best_s, expert = _topk_rows(cand, PEER_TOPK, payload=cidx)
        e = jnp.exp(best_s - best_s[0:1])
        gate_rows.append(e / jnp.sum(e, axis=0, keepdims=True))
        idx_rows.append(expert * np.float32(ROW_SUBLANES))
    idx_ref[...] = jnp.concatenate(idx_rows, axis=0).T.astype(I32)
    gate_ref[...] = jnp.concatenate(gate_rows, axis=0).T


def _peer_topk(h2_split, wq_bf16, sk_hi, sk_lo, tm=256):
    n = h2_split.shape[0] // SUBLANES
    qw = wq_bf16.shape[1]
    out = pl.BlockSpec((tm, PEER_PICKS), lambda i: (i, 0))
    return pl.pallas_call(
        _peer_topk_kernel,
        grid=(n // tm,),
        in_specs=[_row_split_spec(tm),
                  _full((D_MODEL, qw)),
                  _full((2 * PEER_HEADS, PEER_N_KEYS, PEER_HALF)),
                  _full((2 * PEER_HEADS, PEER_N_KEYS, PEER_HALF))],
        out_specs=[out, out],
        out_shape=[jax.ShapeDtypeStruct((n, PEER_PICKS), I32), jax.ShapeDtypeStruct((n, PEER_PICKS), F32)],
        compiler_params=_params("parallel"),
        name="peer_topk",
    )(h2_split, wq_bf16, sk_hi, sk_lo)


def _pack_table(tab):
    lo16 = lax.bitcast_convert_type(tab[:, :ROW_WORDS].astype(BF16), jnp.uint16).astype(jnp.uint32)
    full = lax.bitcast_convert_type(tab[:, ROW_WORDS:], jnp.uint32)
    sign = full & jnp.uint32(0x80000000)
    mag = (full & jnp.uint32(0x7FFFFFFF)) + jnp.uint32(0x8000)
    hi16 = jnp.where(mag >= lo16, (mag - lo16) >> 16, jnp.uint32(0))
    word = sign | (hi16 << 16) | lo16
    return lax.bitcast_convert_type(word, I32).reshape(tab.shape[0] * ROW_SUBLANES, LANES)


def _gather_row(tab_ref, e4):
    w = tab_ref[pl.ds(pl.multiple_of(e4, ROW_SUBLANES), ROW_SUBLANES), :]
    return lax.bitcast_convert_type(w << 16, F32), lax.bitcast_convert_type(w, F32)


def _lane_sums(rows):
    return jnp.dot(rows.astype(BF16), jnp.ones((LANES, LANES), BF16), preferred_element_type=F32)


def _pick_diag():
    return (lax.broadcasted_iota(I32, (PEER_PICKS, LANES), 0)
            == lax.broadcasted_iota(I32, (PEER_PICKS, LANES), 1))


TOKENS_PER_GROUP = 32


def _peer_u_kernel(idx_ref, h_ref, gate_ref, tab_ref, act_ref, slot_ref):
    tm = gate_ref.shape[0]
    slot_rows = PEER_PICKS * ROW_SUBLANES
    diag = _pick_diag()

    def group(g, carry):
        def token(tt, carry2):
            t = g * TOKENS_PER_GROUP + tt
            hv = h_ref[pl.ds(pl.multiple_of(t * SUBLANES, SUBLANES), SUBLANES), :]
            h_lo, h_hi = hv[0:ROW_SUBLANES], hv[ROW_SUBLANES:SUBLANES]
            base = pl.multiple_of(tt * slot_rows, slot_rows)
            for k in range(PEER_PICKS):
                lo, hi = _gather_row(tab_ref, idx_ref[t, k])
                slot_ref[pl.ds(base + k * ROW_SUBLANES, ROW_SUBLANES), :] = lo * h_lo + hi * h_hi
            return carry2

        lax.fori_loop(0, TOKENS_PER_GROUP, token, 0)
        rows = []
        for tt in range(TOKENS_PER_GROUP):
            r = slot_ref[pl.ds(tt * slot_rows, PEER_PICKS, stride=ROW_SUBLANES), :]
            for s in range(1, ROW_SUBLANES):
                r = r + slot_ref[pl.ds(tt * slot_rows + s, PEER_PICKS, stride=ROW_SUBLANES), :]
            rows.append(jnp.sum(jnp.where(diag, _lane_sums(r), 0.0), axis=0, keepdims=True))
        off = pl.multiple_of(g * TOKENS_PER_GROUP, TOKENS_PER_GROUP)
        a = jnp.concatenate(rows, axis=0)
        act_ref[pl.ds(off, TOKENS_PER_GROUP), :] = _gelu(a) * gate_ref[pl.ds(off, TOKENS_PER_GROUP), :]
        return carry

    lax.fori_loop(0, tm // TOKENS_PER_GROUP, group, 0)


def _peer_u(idx, h2_split, gate, tab, tm=128):
    n = idx.shape[0]
    return pl.pallas_call(
        _peer_u_kernel,
        grid=(n // tm,),
        in_specs=[pl.BlockSpec((tm, PEER_PICKS), lambda i: (i, 0), memory_space=pltpu.SMEM),
                  _row_split_spec(tm),
                  pl.BlockSpec((tm, PEER_PICKS), lambda i: (i, 0)),
                  pl.BlockSpec(memory_space=pltpu.VMEM)],
        out_specs=pl.BlockSpec((tm, PEER_PICKS), lambda i: (i, 0)),
        out_shape=jax.ShapeDtypeStruct((n, PEER_PICKS), F32),
        scratch_shapes=[pltpu.VMEM((TOKENS_PER_GROUP * PEER_PICKS * ROW_SUBLANES, LANES), F32)],
        compiler_params=_params("arbitrary"),
        name="peer_u",
    )(idx, h2_split, gate, tab)


N_ACC = 4


def _peer_v_kernel(idx_ref, act_ref, x_ref, g_ref, tab_ref, out_ref, actb_ref, x2_ref, *, normalize):
    tm = idx_ref.shape[0]
    diag = _pick_diag()

    def group(g, carry):
        off = pl.multiple_of(g * TOKENS_PER_GROUP, TOKENS_PER_GROUP)
        for tt in range(TOKENS_PER_GROUP):
            spread = jnp.where(diag, act_ref[pl.ds(off + tt, 1), :], 0.0)
            actb_ref[tt * PEER_PICKS:(tt + 1) * PEER_PICKS, :] = _lane_sums(spread)

        def token(tt, carry2):
            t = g * TOKENS_PER_GROUP + tt
            base = pl.multiple_of(tt * PEER_PICKS, PEER_PICKS)
            acc_lo = [jnp.zeros((ROW_SUBLANES, LANES), F32) for _ in range(N_ACC)]
            acc_hi = [jnp.zeros((ROW_SUBLANES, LANES), F32) for _ in range(N_ACC)]
            for k in range(PEER_PICKS):
                lo, hi = _gather_row(tab_ref, idx_ref[t, k])
                a = actb_ref[pl.ds(base + k, 1), :]
                acc_lo[k % N_ACC] = acc_lo[k % N_ACC] + a * lo
                acc_hi[k % N_ACC] = acc_hi[k % N_ACC] + a * hi
            lo = (acc_lo[0] + acc_lo[1]) + (acc_lo[2] + acc_lo[3])
            hi = (acc_hi[0] + acc_hi[1]) + (acc_hi[2] + acc_hi[3])
            r0 = pl.multiple_of(t * SUBLANES, SUBLANES)
            r1 = pl.multiple_of(t * SUBLANES + ROW_SUBLANES, ROW_SUBLANES)
            x2_ref[pl.ds(r0, ROW_SUBLANES), :] = x_ref[pl.ds(r0, ROW_SUBLANES), :] + lo
            x2_ref[pl.ds(r1, ROW_SUBLANES), :] = x_ref[pl.ds(r1, ROW_SUBLANES), :] + hi
            return carry2

        lax.fori_loop(0, TOKENS_PER_GROUP, token, 0)
        return carry

    lax.fori_loop(0, tm // TOKENS_PER_GROUP, group, 0)
    x2 = _load_row_split(x2_ref)
    out_ref[...] = _rms(x2, g_ref[...]) if normalize else x2


def _peer_v(idx, act, x1_split, tab, final_g, tm=128):
    n = idx.shape[0]
    normalize = final_g is not None
    g = final_g if normalize else jnp.ones((1, D_MODEL), F32)
    return pl.pallas_call(
        functools.partial(_peer_v_kernel, normalize=normalize),
        grid=(n // tm,),
        in_specs=[pl.BlockSpec((tm, PEER_PICKS), lambda i: (i, 0), memory_space=pltpu.SMEM),
                  pl.BlockSpec((tm, PEER_PICKS), lambda i: (i, 0)),
                  _row_split_spec(tm), _full((1, D_MODEL)), pl.BlockSpec(memory_space=pltpu.VMEM)],
        out_specs=pl.BlockSpec((tm, D_MODEL), lambda i: (i, 0)),
        out_shape=jax.ShapeDtypeStruct((n, D_MODEL), F32),
        scratch_shapes=[pltpu.VMEM((TOKENS_PER_GROUP * PEER_PICKS, LANES), F32),
                        pltpu.VMEM((tm * SUBLANES, LANES), F32)],
        compiler_params=_params("arbitrary"),
        name="peer_v",
    )(idx, act, x1_split, g, tab)


def kernel(x, norm_mix_g, w_in, sgu_norm_g, sgu_w, sgu_b, w_branch_a, w_branch_b, w_out,
           norm_ffn_g, peer_wq, peer_subkeys, peer_u, peer_v, norm_final_g):
    b, s, d = x.shape
    n = b * s
    x2 = x.reshape(n, d)
    for l in range(w_in.shape[0]):
        u, v, ga, gb, *qkv_views = _in_proj(x2, norm_mix_g[l].reshape(1, d), w_in[l].astype(BF16))
        bs_b = jnp.broadcast_to(sgu_b[l][:, :, None], (SGU_GROUPS, SGU_CHUNK, LANES))
        pa = _sgu(u, v, ga, sgu_norm_g[l].reshape(1, SGU_WIDTH), sgu_w[l].astype(BF16), bs_b,
                  w_branch_a[l].astype(BF16))
        os_, ls_ = [], []
        for g, (window, dilation) in enumerate(ATT_PATTERNS):
            qv, kv, vv = qkv_views[3 * g:3 * g + 3]
            o, lse = _attention_pattern(qv, kv, vv, b, dilation, window // (2 * dilation))
            os_.append(o)
            ls_.append(lse)
        x1, h2 = _merge(x2, pa, gb, os_, ls_, w_branch_b[l].astype(BF16), w_out[l].astype(BF16),
                        norm_ffn_g[l].reshape(1, d))
        sk = peer_subkeys[l].reshape(2 * PEER_HEADS, PEER_N_KEYS, PEER_HALF)
        sk_hi = sk.astype(BF16)
        sk_lo = (sk - sk_hi.astype(F32)).astype(BF16)
        idx, gate = _peer_topk(h2, peer_wq[l].astype(BF16), sk_hi, sk_lo)
        act = _peer_u(idx, h2, gate, _pack_table(peer_u[l]))
        last = l == w_in.shape[0] - 1
        x2 = _peer_v(idx, act, x1, _pack_table(peer_v[l]), norm_final_g.reshape(1, d) if last else None)
    return x2.reshape(b, s, d)
```

```python
import functools

import numpy as np
import jax
import jax.numpy as jnp
from jax import lax
from jax.experimental import pallas as pl
from jax.experimental.pallas import tpu as pltpu

F32 = jnp.float32
BF16 = jnp.bfloat16
I32 = jnp.int32

D_MODEL = 1024
EPS = 1e-6
NEG_INF = -1e30
SGU_CHUNK = 128
SGU_GROUPS = 8
SGU_WIDTH = 1024
ATT_HEADS = 8
ATT_HEAD_DIM = 64
ATT_PATTERNS = ((128, 1), (512, 4), (2048, 16))
ATT_GROUPS = 3
ATT_KV_WIDTH = ATT_HEADS * ATT_HEAD_DIM
ATT_Q_WIDTH = ATT_GROUPS * ATT_KV_WIDTH
PEER_HEADS = 8
PEER_N_KEYS = 128
PEER_N_EXPERTS = PEER_N_KEYS * PEER_N_KEYS
PEER_HALF = 128
PEER_TOPK = 16
PEER_PICKS = PEER_HEADS * PEER_TOPK

LANES = 128
SUBLANES = 8
ROW_WORDS = D_MODEL // 2
ROW_SUBLANES = ROW_WORDS // LANES
VMEM_LIMIT = 56 * 1024 * 1024

IN_CHUNK = 512
ATT_DILATIONS = tuple(d for _, d in ATT_PATTERNS)
_N_PLAIN = 4
_IN_LAYOUT = (
    ("gelu", (0, 0), ()), ("gelu", (0, 1), ()), ("gelu", (1, 0), ()), ("gelu", (1, 1), ()),
    *((None, None, ((_N_PLAIN + 3 * g, d),)) for g, d in enumerate(ATT_DILATIONS)),
    (None, None, tuple((_N_PLAIN + 3 * g + 1, d) for g, d in enumerate(ATT_DILATIONS))),
    (None, None, tuple((_N_PLAIN + 3 * g + 2, d) for g, d in enumerate(ATT_DILATIONS))),
    ("sigmoid", (2, 0), ()), ("sigmoid", (2, 1), ()), ("sigmoid", (3, 0), ()), ("sigmoid", (3, 1), ()),
)


def _params(*sem):
    return pltpu.CompilerParams(dimension_semantics=sem, vmem_limit_bytes=VMEM_LIMIT)


def _gelu(y):
    return 0.5 * y * (1.0 + lax.erf(y * np.float32(1.0 / np.sqrt(2.0))))


def _rms(x, g):
    return x * lax.rsqrt(jnp.mean(x * x, axis=-1, keepdims=True) + EPS) * g


def _full(shape):
    return pl.BlockSpec(shape, lambda *_: (0,) * len(shape))


def _store_row_split(ref, x):
    tm = x.shape[0]
    for s in range(SUBLANES):
        ref[pl.ds(s, tm, stride=SUBLANES), :] = x[:, s * LANES:(s + 1) * LANES]


def _load_row_split(ref):
    tm = ref.shape[0] // SUBLANES
    return jnp.concatenate([ref[pl.ds(s, tm, stride=SUBLANES), :] for s in range(SUBLANES)], axis=1)


def _row_split_spec(tm):
    return pl.BlockSpec((tm * SUBLANES, LANES), lambda i: (i, 0))


def _in_proj_kernel(x_ref, g_ref, w_ref, *refs):
    out_refs, ybuf_ref = refs[:-1], refs[-1]
    h = _rms(x_ref[...], g_ref[...]).astype(BF16)
    for c, (epi, plain, views) in enumerate(_IN_LAYOUT):
        y = jnp.dot(h, w_ref[:, c * IN_CHUNK:(c + 1) * IN_CHUNK], preferred_element_type=F32)
        if epi == "gelu":
            y = _gelu(y)
        elif epi == "sigmoid":
            y = jax.nn.sigmoid(y)
        if plain is not None:
            slot, sub = plain
            out_refs[slot][:, sub * IN_CHUNK:(sub + 1) * IN_CHUNK] = y.astype(BF16)
        if any(d > 1 for _, d in views):
            for j in range(IN_CHUNK // LANES):
                ybuf_ref[j] = y[:, j * LANES:(j + 1) * LANES]
        for slot, d in views:
            ref = out_refs[slot]
            if d == 1:
                ref[...] = y.astype(BF16)
                continue
            for r in range(d):
                for j in range(IN_CHUNK // LANES):
                    c0 = r * IN_CHUNK + j * LANES
                    ref[:, c0:c0 + LANES] = ybuf_ref[j, pl.ds(r, ref.shape[0], stride=d), :].astype(BF16)


def _in_proj(x2d, g, w_bf16, tm=512):
    n = x2d.shape[0]
    width = w_bf16.shape[1]
    shapes = [(n, SGU_WIDTH), (n, SGU_WIDTH), (n, D_MODEL), (n, D_MODEL)]
    blocks = [(tm, SGU_WIDTH), (tm, SGU_WIDTH), (tm, D_MODEL), (tm, D_MODEL)]
    for d in ATT_DILATIONS:
        shapes += [(n // d, d * IN_CHUNK)] * 3
        blocks += [(tm // d, d * IN_CHUNK)] * 3
    return pl.pallas_call(
        _in_proj_kernel,
        grid=(n // tm,),
        in_specs=[pl.BlockSpec((tm, D_MODEL), lambda i: (i, 0)),
                  _full((1, D_MODEL)),
                  pl.BlockSpec((D_MODEL, width), lambda i: (0, 0), pipeline_mode=pl.Buffered(1))],
        out_specs=[pl.BlockSpec(blk, lambda i: (i, 0)) for blk in blocks],
        out_shape=[jax.ShapeDtypeStruct(shp, BF16) for shp in shapes],
        scratch_shapes=[pltpu.VMEM((IN_CHUNK // LANES, tm, LANES), F32)],
        compiler_params=_params("parallel"),
        name="in_proj",
    )(x2d, g, w_bf16)


def _sgu_kernel(u_ref, v_ref, ga_ref, ng_ref, ws_ref, bs_ref, wa_ref, out_ref):
    tm = u_ref.shape[0]
    vn = _rms(v_ref[...].astype(F32), ng_ref[...]).astype(BF16)
    chunks = []
    for c in range(tm // SGU_CHUNK):
        cols = []
        for g in range(SGU_GROUPS):
            blk = vn[c * SGU_CHUNK:(c + 1) * SGU_CHUNK, g * LANES:(g + 1) * LANES]
            cols.append(jnp.dot(ws_ref[g], blk, preferred_element_type=F32) + bs_ref[g])
        chunks.append(jnp.concatenate(cols, axis=1))
    mixed = jnp.concatenate(chunks, axis=0)
    ya = (u_ref[...].astype(F32) * mixed).astype(BF16)
    pa = jnp.dot(ya, wa_ref[...], preferred_element_type=F32) * ga_ref[...].astype(F32)
    out_ref[...] = pa.astype(BF16)


def _sgu(u, v, ga, ng, ws_bf16, bs_b, wa_bf16, tm=512):
    n = u.shape[0]
    row = pl.BlockSpec((tm, SGU_WIDTH), lambda i: (i, 0))
    return pl.pallas_call(
        _sgu_kernel,
        grid=(n // tm,),
        in_specs=[row, row, row, _full((1, SGU_WIDTH)),
                  _full((SGU_GROUPS, SGU_CHUNK, SGU_CHUNK)),
                  _full((SGU_GROUPS, SGU_CHUNK, LANES)),
                  _full((SGU_WIDTH, D_MODEL))],
        out_specs=pl.BlockSpec((tm, D_MODEL), lambda i: (i, 0)),
        out_shape=jax.ShapeDtypeStruct((n, D_MODEL), BF16),
        compiler_params=_params("parallel"),
        name="sgu",
    )(u, v, ga, ng, ws_bf16, bs_b, wa_bf16)


def _attn_kernel(q_ref, kp_ref, kc_ref, kn_ref, vp_ref, vc_ref, vn_ref, o_ref, lse_ref,
                 *, dilation, n_side, sub_len):
    tq = q_ref.shape[0]
    win = tq + 2 * n_side
    i = pl.program_id(2)
    k = jnp.concatenate([kp_ref[...], kc_ref[...], kn_ref[...]], axis=0)
    v = jnp.concatenate([vp_ref[...], vc_ref[...], vn_ref[...]], axis=0)
    row = lax.broadcasted_iota(I32, (tq, win), 0)
    col = lax.broadcasted_iota(I32, (tq, win), 1)
    rel = col - n_side - row
    pos = i * tq - n_side + col
    valid = (jnp.abs(rel) <= n_side) & (pos >= 0) & (pos < sub_len)
    negdist = -(jnp.abs(rel) * dilation).astype(F32)
    lane = lax.broadcasted_iota(I32, (tq, LANES), 1)
    low_half = lane < ATT_HEAD_DIM
    scale = np.float32(ATT_HEAD_DIM ** -0.5)
    for pair in range(ATT_HEADS // 2):
        sl = slice(pair * LANES, (pair + 1) * LANES)
        qp, kpair, vpair = q_ref[:, sl], k[:, sl], v[:, sl]
        outs, lses = [], []
        for j in range(2):
            slope = np.float32(2.0 ** (-8.0 * (pair * 2 + j + 1) / ATT_HEADS))
            qm = jnp.where(low_half if j == 0 else jnp.logical_not(low_half), qp, jnp.zeros_like(qp))
            s = lax.dot_general(qm, kpair, (((1,), (1,)), ((), ())), preferred_element_type=F32)
            s = jnp.where(valid, s * scale + slope * negdist, NEG_INF)
            m = jnp.max(s, axis=-1, keepdims=True)
            p = jnp.exp(s - m)
            l = jnp.sum(p, axis=-1, keepdims=True)
            o = jnp.dot(p.astype(BF16), vpair, preferred_element_type=F32)
            outs.append(o / l)
            lses.append(jnp.broadcast_to(m + jnp.log(l), (tq, LANES)))
        o_ref[:, sl] = jnp.where(low_half, outs[0], outs[1]).astype(o_ref.dtype)
        lse_ref[:, sl] = jnp.where(low_half, lses[0], lses[1])


def _attention_pattern(qv, kv, vv, batch, dilation, n_side, tq=128):
    sub_len = qv.shape[0] // batch
    kvw = ATT_KV_WIDTH
    b = batch
    qv, kv, vv = (t.reshape(b, sub_len, dilation * kvw) for t in (qv, kv, vv))
    per = tq // n_side
    last = sub_len // n_side - 1
    cur = pl.BlockSpec((None, tq, kvw), lambda bi, r, i: (bi, i, r))
    prev = pl.BlockSpec((None, n_side, kvw), lambda bi, r, i: (bi, jnp.maximum(i * per - 1, 0), r))
    nxt = pl.BlockSpec((None, n_side, kvw), lambda bi, r, i: (bi, jnp.minimum((i + 1) * per, last), r))
    o, lse = pl.pallas_call(
        functools.partial(_attn_kernel, dilation=dilation, n_side=n_side, sub_len=sub_len),
        grid=(b, dilation, sub_len // tq),
        in_specs=[cur, prev, cur, nxt, prev, cur, nxt],
        out_specs=[cur, cur],
        out_shape=[jax.ShapeDtypeStruct((b, sub_len, dilation * kvw), BF16),
                   jax.ShapeDtypeStruct((b, sub_len, dilation * kvw), F32)],
        compiler_params=_params("parallel", "parallel", "parallel"),
        name=f"attn_d{dilation}",
    )(qv, kv, kv, kv, vv, vv, vv)
    return o.reshape(b * sub_len, dilation * kvw), lse.reshape(b * sub_len, dilation * kvw)


def _from_residue_view(blk_ref, tmp_ref, d):
    if d == 1:
        return blk_ref[...].astype(F32)
    n_tiles = IN_CHUNK // LANES
    for r in range(d):
        for j in range(n_tiles):
            c0 = r * IN_CHUNK + j * LANES
            tmp_ref[j, pl.ds(r, blk_ref.shape[0], stride=d), :] = blk_ref[:, c0:c0 + LANES].astype(F32)
    return jnp.concatenate([tmp_ref[j] for j in range(n_tiles)], axis=1)


def _merge_kernel(x_ref, pa_ref, gb_ref, o1_ref, o2_ref, o3_ref, l1_ref, l2_ref, l3_ref,
                  wb_ref, wo_ref, g_ref, x1_ref, h2_ref, *tmp_refs):
    tmp = iter(tmp_refs)
    os_, ls_ = [], []
    for o_ref, l_ref, d in zip((o1_ref, o2_ref, o3_ref), (l1_ref, l2_ref, l3_ref), ATT_DILATIONS):
        os_.append(_from_residue_view(o_ref, next(tmp) if d > 1 else None, d))
        ls_.append(_from_residue_view(l_ref, next(tmp) if d > 1 else None, d))
    l1, l2, l3 = ls_
    m = jnp.maximum(jnp.maximum(l1, l2), l3)
    w1, w2, w3 = jnp.exp(l1 - m), jnp.exp(l2 - m), jnp.exp(l3 - m)
    yb = (w1 * os_[0] + w2 * os_[1] + w3 * os_[2]) / (w1 + w2 + w3)
    pb = jnp.dot(yb.astype(BF16), wb_ref[...], preferred_element_type=F32) * gb_ref[...].astype(F32)
    merged = (pa_ref[...].astype(F32) + pb).astype(BF16)
    x1 = x_ref[...] + jnp.dot(merged, wo_ref[...], preferred_element_type=F32)
    _store_row_split(x1_ref, x1)
    _store_row_split(h2_ref, _rms(x1, g_ref[...]))


def _merge(x2d, pa, gb, os_, ls_, wb_bf16, wo_bf16, g, tm=512):
    n = x2d.shape[0]
    wide = pl.BlockSpec((tm, D_MODEL), lambda i: (i, 0))
    views = [pl.BlockSpec((tm // d, d * ATT_KV_WIDTH), lambda i: (i, 0)) for d in ATT_DILATIONS]
    n_tmp = 2 * sum(d > 1 for d in ATT_DILATIONS)
    return pl.pallas_call(
        _merge_kernel,
        grid=(n // tm,),
        in_specs=[wide, wide, wide, *views, *views,
                  _full((ATT_KV_WIDTH, D_MODEL)), _full((D_MODEL, D_MODEL)), _full((1, D_MODEL))],
        out_specs=[_row_split_spec(tm), _row_split_spec(tm)],
        out_shape=[jax.ShapeDtypeStruct((n * SUBLANES, LANES), F32)] * 2,
        scratch_shapes=[pltpu.VMEM((ATT_KV_WIDTH // LANES, tm, LANES), F32)] * n_tmp,
        compiler_params=_params("parallel"),
        name="merge",
    )(x2d, pa, gb, *os_, *ls_, wb_bf16, wo_bf16, g)


def _topk_rows(s, k, payload=None):
    rows = lax.broadcasted_iota(I32, s.shape, 0).astype(F32)
    vals, ids = [], []
    for _ in range(k):
        m = jnp.max(s, axis=0, keepdims=True)
        am = jnp.min(jnp.where(s == m, rows, np.float32(s.shape[0])), axis=0, keepdims=True)
        hit = rows == am
        vals.append(m)
        if payload is None:
            ids.append(am)
        else:
            ids.append(jnp.max(jnp.where(hit, payload, -1.0), axis=0, keepdims=True))
        s = jnp.where(hit, -jnp.inf, s)
    return jnp.concatenate(vals, axis=0), jnp.concatenate(ids, axis=0)


def _pair_candidates(s0, s1, i0, i1):
    vals, ids = [], []
    for a in range(SUBLANES):
        n_b = PEER_TOPK // (a + 1)
        rows = PEER_TOPK if n_b > SUBLANES else SUBLANES
        v = s0[a:a + 1] + s1[0:rows]
        if n_b < rows:
            v = jnp.where(lax.broadcasted_iota(I32, v.shape, 0) < n_b, v, -jnp.inf)
        vals.append(v)
        ids.append(i0[a:a + 1] * PEER_N_KEYS + i1[0:rows])
    vals.append(s0[SUBLANES:PEER_TOPK] + s1[0:1])
    ids.append(i0[SUBLANES:PEER_TOPK] * PEER_N_KEYS + i1[0:1])
    return jnp.concatenate(vals, axis=0), jnp.concatenate(ids, axis=0)


def _split_bf16(a):
    hi = a.astype(BF16)
    return hi, (a - hi.astype(F32)).astype(BF16)


def _peer_topk_kernel(h_ref, wq_ref, skh_ref, skl_ref, idx_ref, gate_ref):
    q = jnp.dot(_load_row_split(h_ref).astype(BF16), wq_ref[...], preferred_element_type=F32)
    nt = (((1,), (1,)), ((), ()))
    idx_rows, gate_rows = [], []
    for hd in range(PEER_HEADS):
        top_s, top_i = [], []
        for p in range(2):
            hp = hd * 2 + p
            q_hi, q_lo = _split_bf16(q[:, hp * PEER_HALF:(hp + 1) * PEER_HALF])
            kh, kl = skh_ref[hp], skl_ref[hp]
            st = (lax.dot_general(kh, q_hi, nt, preferred_element_type=F32)
                  + lax.dot_general(kh, q_lo, nt, preferred_element_type=F32)
                  + lax.dot_general(kl, q_hi, nt, preferred_element_type=F32))
            ts, ti = _topk_rows(st, PEER_TOPK)
            top_s.append(ts)
            top_i.append(ti)
        cand, cidx = _pair_candidates(top_s[0], top_s[1], top_i[0], top_i[1])
        best_s, expert = _topk_rows(cand, PEER_TOPK, payload=cidx)
        e = jnp.exp(best_s - best_s[0:1])
        gate_rows.append(e / jnp.sum(e, axis=0, keepdims=True))
        idx_rows.append(expert * np.float32(ROW_SUBLANES))
    idx_ref[...] = jnp.concatenate(idx_rows, axis=0).T.astype(I32)
    gate_ref[...] = jnp.concatenate(gate_rows, axis=0).T


def _peer_topk(h2_split, wq_bf16, sk_hi, sk_lo, tm=256):
    n = h2_split.shape[0] // SUBLANES
    qw = wq_bf16.shape[1]
    out = pl.BlockSpec((tm, PEER_PICKS), lambda i: (i, 0))
    return pl.pallas_call(
        _peer_topk_kernel,
        grid=(n // tm,),
        in_specs=[_row_split_spec(tm),
                  _full((D_MODEL, qw)),
                  _full((2 * PEER_HEADS, PEER_N_KEYS, PEER_HALF)),
                  _full((2 * PEER_HEADS, PEER_N_KEYS, PEER_HALF))],
        out_specs=[out, out],
        out_shape=[jax.ShapeDtypeStruct((n, PEER_PICKS), I32), jax.ShapeDtypeStruct((n, PEER_PICKS), F32)],
        compiler_params=_params("parallel"),
        name="peer_topk",
    )(h2_split, wq_bf16, sk_hi, sk_lo)


def _pack_table(tab):
    lo16 = lax.bitcast_convert_type(tab[:, :ROW_WORDS].astype(BF16), jnp.uint16).astype(jnp.uint32)
    full = lax.bitcast_convert_type(tab[:, ROW_WORDS:], jnp.uint32)
    sign = full & jnp.uint32(0x80000000)
    mag = (full & jnp.uint32(0x7FFFFFFF)) + jnp.uint32(0x8000)
    hi16 = jnp.where(mag >= lo16, (mag - lo16) >> 16, jnp.uint32(0))
    word = sign | (hi16 << 16) | lo16
    return lax.bitcast_convert_type(word, I32).reshape(tab.shape[0] * ROW_SUBLANES, LANES)


def _gather_row(tab_ref, e4):
    w = tab_ref[pl.ds(pl.multiple_of(e4, ROW_SUBLANES), ROW_SUBLANES), :]
    return lax.bitcast_convert_type(w << 16, F32), lax.bitcast_convert_type(w, F32)


def _lane_sums(rows):
    return jnp.dot(rows.astype(BF16), jnp.ones((LANES, LANES), BF16), preferred_element_type=F32)


def _pick_diag():
    return (lax.broadcasted_iota(I32, (PEER_PICKS, LANES), 0)
            == lax.broadcasted_iota(I32, (PEER_PICKS, LANES), 1))


STAGE_TOKENS = 16
TOKENS_PER_GROUP = 2 * STAGE_TOKENS


def _stage_copy(idx_ref, stage, bufs, sems, b):
    rows = pl.ds(pl.multiple_of(stage * STAGE_TOKENS, STAGE_TOKENS), STAGE_TOKENS)
    return pltpu.make_async_copy(idx_ref.at[rows], bufs[b], sems.at[b])


def _for_group_tokens(g, n_stages, idx_ref, bufs, sems, token_fn):
    for b in range(2):
        stage = 2 * g + b
        _stage_copy(idx_ref, stage, bufs, sems, b).wait()
        for tt in range(STAGE_TOKENS):
            local = b * STAGE_TOKENS + tt
            token_fn(local, g * TOKENS_PER_GROUP + local, lambda k, b=b, tt=tt: bufs[b][tt, k])

        @pl.when(stage + 2 < n_stages)
        def _():
            _stage_copy(idx_ref, stage + 2, bufs, sems, b).start()


def _start_first_stages(idx_ref, bufs, sems):
    for b in range(2):
        _stage_copy(idx_ref, b, bufs, sems, b).start()


def _peer_u_kernel(idx_ref, h_ref, gate_ref, tab_ref, act_ref, slot_ref, stage_a, stage_b, sems):
    tm = gate_ref.shape[0]
    slot_rows = PEER_PICKS * ROW_SUBLANES
    diag = _pick_diag()
    bufs = (stage_a, stage_b)
    _start_first_stages(idx_ref, bufs, sems)

    def token(local, t, pick_offset):
        hv = h_ref[pl.ds(pl.multiple_of(t * SUBLANES, SUBLANES), SUBLANES), :]
        h_lo, h_hi = hv[0:ROW_SUBLANES], hv[ROW_SUBLANES:SUBLANES]
        for k in range(PEER_PICKS):
            lo, hi = _gather_row(tab_ref, pick_offset(k))
            r0 = local * slot_rows + k * ROW_SUBLANES
            slot_ref[r0:r0 + ROW_SUBLANES, :] = lo * h_lo + hi * h_hi

    def group(g, carry):
        _for_group_tokens(g, tm // STAGE_TOKENS, idx_ref, bufs, sems, token)
        rows = []
        for tt in range(TOKENS_PER_GROUP):
            r = slot_ref[pl.ds(tt * slot_rows, PEER_PICKS, stride=ROW_SUBLANES), :]
            for s in range(1, ROW_SUBLANES):
                r = r + slot_ref[pl.ds(tt * slot_rows + s, PEER_PICKS, stride=ROW_SUBLANES), :]
            rows.append(jnp.sum(jnp.where(diag, _lane_sums(r), 0.0), axis=0, keepdims=True))
        off = pl.multiple_of(g * TOKENS_PER_GROUP, TOKENS_PER_GROUP)
        a = jnp.concatenate(rows, axis=0)
        act_ref[pl.ds(off, TOKENS_PER_GROUP), :] = _gelu(a) * gate_ref[pl.ds(off, TOKENS_PER_GROUP), :]
        return carry

    lax.fori_loop(0, tm // TOKENS_PER_GROUP, group, 0)


def _stage_scratch():
    stage = pltpu.SMEM((STAGE_TOKENS, PEER_PICKS), I32)
    return [stage, stage, pltpu.SemaphoreType.DMA((2,))]


def _peer_u(idx, h2_split, gate, tab, tm=128):
    n = idx.shape[0]
    picks = pl.BlockSpec((tm, PEER_PICKS), lambda i: (i, 0))
    return pl.pallas_call(
        _peer_u_kernel,
        grid=(n // tm,),
        in_specs=[picks, _row_split_spec(tm), picks, pl.BlockSpec(memory_space=pltpu.VMEM)],
        out_specs=picks,
        out_shape=jax.ShapeDtypeStruct((n, PEER_PICKS), F32),
        scratch_shapes=[pltpu.VMEM((TOKENS_PER_GROUP * PEER_PICKS * ROW_SUBLANES, LANES), F32), *_stage_scratch()],
        compiler_params=_params("arbitrary"),
        name="peer_u",
    )(idx, h2_split, gate, tab)


N_ACC = 4


def _peer_v_kernel(idx_ref, act_ref, x_ref, g_ref, tab_ref, out_ref, actb_ref, x2_ref, stage_a, stage_b, sems,
                   *, normalize):
    tm = idx_ref.shape[0]
    diag = _pick_diag()
    bufs = (stage_a, stage_b)
    _start_first_stages(idx_ref, bufs, sems)

    def token(local, t, pick_offset):
        acc_lo = [jnp.zeros((ROW_SUBLANES, LANES), F32) for _ in range(N_ACC)]
        acc_hi = [jnp.zeros((ROW_SUBLANES, LANES), F32) for _ in range(N_ACC)]
        for k in range(PEER_PICKS):
            lo, hi = _gather_row(tab_ref, pick_offset(k))
            a = actb_ref[local * PEER_PICKS + k:local * PEER_PICKS + k + 1, :]
            acc_lo[k % N_ACC] = acc_lo[k % N_ACC] + a * lo
            acc_hi[k % N_ACC] = acc_hi[k % N_ACC] + a * hi
        lo = (acc_lo[0] + acc_lo[1]) + (acc_lo[2] + acc_lo[3])
        hi = (acc_hi[0] + acc_hi[1]) + (acc_hi[2] + acc_hi[3])
        r0 = pl.multiple_of(t * SUBLANES, SUBLANES)
        r1 = pl.multiple_of(t * SUBLANES + ROW_SUBLANES, ROW_SUBLANES)
        x2_ref[pl.ds(r0, ROW_SUBLANES), :] = x_ref[pl.ds(r0, ROW_SUBLANES), :] + lo
        x2_ref[pl.ds(r1, ROW_SUBLANES), :] = x_ref[pl.ds(r1, ROW_SUBLANES), :] + hi

    def group(g, carry):
        off = pl.multiple_of(g * TOKENS_PER_GROUP, TOKENS_PER_GROUP)
        for tt in range(TOKENS_PER_GROUP):
            spread = jnp.where(diag, act_ref[pl.ds(off + tt, 1), :], 0.0)
            actb_ref[tt * PEER_PICKS:(tt + 1) * PEER_PICKS, :] = _lane_sums(spread)
        _for_group_tokens(g, tm // STAGE_TOKENS, idx_ref, bufs, sems, token)
        return carry

    lax.fori_loop(0, tm // TOKENS_PER_GROUP, group, 0)
    x2 = _load_row_split(x2_ref)
    out_ref[...] = _rms(x2, g_ref[...]) if normalize else x2


def _peer_v(idx, act, x1_split, tab, final_g, tm=128):
    n = idx.shape[0]
    normalize = final_g is not None
    g = final_g if normalize else jnp.ones((1, D_MODEL), F32)
    picks = pl.BlockSpec((tm, PEER_PICKS), lambda i: (i, 0))
    return pl.pallas_call(
        functools.partial(_peer_v_kernel, normalize=normalize),
        grid=(n // tm,),
        in_specs=[picks, picks, _row_split_spec(tm), _full((1, D_MODEL)), pl.BlockSpec(memory_space=pltpu.VMEM)],
        out_specs=pl.BlockSpec((tm, D_MODEL), lambda i: (i, 0)),
        out_shape=jax.ShapeDtypeStruct((n, D_MODEL), F32),
        scratch_shapes=[pltpu.VMEM((TOKENS_PER_GROUP * PEER_PICKS, LANES), F32),
                        pltpu.VMEM((tm * SUBLANES, LANES), F32), *_stage_scratch()],
        compiler_params=_params("arbitrary"),
        name="peer_v",
    )(idx, act, x1_split, g, tab)


def kernel(x, norm_mix_g, w_in, sgu_norm_g, sgu_w, sgu_b, w_branch_a, w_branch_b, w_out,
           norm_ffn_g, peer_wq, peer_subkeys, peer_u, peer_v, norm_final_g):
    b, s, d = x.shape
    n = b * s
    x2 = x.reshape(n, d)
    for l in range(w_in.shape[0]):
        u, v, ga, gb, *qkv_views = _in_proj(x2, norm_mix_g[l].reshape(1, d), w_in[l].astype(BF16))
        bs_b = jnp.broadcast_to(sgu_b[l][:, :, None], (SGU_GROUPS, SGU_CHUNK, LANES))
        pa = _sgu(u, v, ga, sgu_norm_g[l].reshape(1, SGU_WIDTH), sgu_w[l].astype(BF16), bs_b,
                  w_branch_a[l].astype(BF16))
        os_, ls_ = [], []
        for g, (window, dilation) in enumerate(ATT_PATTERNS):
            qv, kv, vv = qkv_views[3 * g:3 * g + 3]
            o, lse = _attention_pattern(qv, kv, vv, b, dilation, window // (2 * dilation))
            os_.append(o)
            ls_.append(lse)
        x1, h2 = _merge(x2, pa, gb, os_, ls_, w_branch_b[l].astype(BF16), w_out[l].astype(BF16),
                        norm_ffn_g[l].reshape(1, d))
        sk = peer_subkeys[l].reshape(2 * PEER_HEADS, PEER_N_KEYS, PEER_HALF)
        sk_hi = sk.astype(BF16)
        sk_lo = (sk - sk_hi.astype(F32)).astype(BF16)
        idx, gate = _peer_topk(h2, peer_wq[l].astype(BF16), sk_hi, sk_lo)
        act = _peer_u(idx, h2, gate, _pack_table(peer_u[l]))
        last = l == w_in.shape[0] - 1
        x2 = _peer_v(idx, act, x1, _pack_table(peer_v[l]), norm_final_g.reshape(1, d) if last else None)
    return x2.reshape(b, s, d)
```

```python
import functools

import numpy as np
import jax
import jax.numpy as jnp
from jax import lax
from jax.experimental import pallas as pl
from jax.experimental.pallas import tpu as pltpu

F32 = jnp.float32
BF16 = jnp.bfloat16
I32 = jnp.int32

D_MODEL = 1024
EPS = 1e-6
NEG_INF = -1e30
SGU_CHUNK = 128
SGU_GROUPS = 8
SGU_WIDTH = 1024
ATT_HEADS = 8
ATT_HEAD_DIM = 64
ATT_PATTERNS = ((128, 1), (512, 4), (2048, 16))
ATT_GROUPS = 3
ATT_KV_WIDTH = ATT_HEADS * ATT_HEAD_DIM
ATT_Q_WIDTH = ATT_GROUPS * ATT_KV_WIDTH
PEER_HEADS = 8
PEER_N_KEYS = 128
PEER_N_EXPERTS = PEER_N_KEYS * PEER_N_KEYS
PEER_HALF = 128
PEER_TOPK = 16
PEER_PICKS = PEER_HEADS * PEER_TOPK

LANES = 128
SUBLANES = 8
ROW_WORDS = D_MODEL // 2
ROW_SUBLANES = ROW_WORDS // LANES
VMEM_LIMIT = 56 * 1024 * 1024

IN_CHUNK = 512
ATT_DILATIONS = tuple(d for _, d in ATT_PATTERNS)
_N_PLAIN = 4
_IN_LAYOUT = (
    ("gelu", (0, 0), ()), ("gelu", (0, 1), ()), ("gelu", (1, 0), ()), ("gelu", (1, 1), ()),
    *((None, None, ((_N_PLAIN + 3 * g, d),)) for g, d in enumerate(ATT_DILATIONS)),
    (None, None, tuple((_N_PLAIN + 3 * g + 1, d) for g, d in enumerate(ATT_DILATIONS))),
    (None, None, tuple((_N_PLAIN + 3 * g + 2, d) for g, d in enumerate(ATT_DILATIONS))),
    ("sigmoid", (2, 0), ()), ("sigmoid", (2, 1), ()), ("sigmoid", (3, 0), ()), ("sigmoid", (3, 1), ()),
)


def _params(*sem):
    return pltpu.CompilerParams(dimension_semantics=sem, vmem_limit_bytes=VMEM_LIMIT)


def _gelu(y):
    return 0.5 * y * (1.0 + lax.erf(y * np.float32(1.0 / np.sqrt(2.0))))


def _rms(x, g):
    return x * lax.rsqrt(jnp.mean(x * x, axis=-1, keepdims=True) + EPS) * g


def _full(shape):
    return pl.BlockSpec(shape, lambda *_: (0,) * len(shape))


def _store_row_split(ref, x):
    tm = x.shape[0]
    for s in range(SUBLANES):
        ref[pl.ds(s, tm, stride=SUBLANES), :] = x[:, s * LANES:(s + 1) * LANES]


def _load_row_split(ref):
    tm = ref.shape[0] // SUBLANES
    return jnp.concatenate([ref[pl.ds(s, tm, stride=SUBLANES), :] for s in range(SUBLANES)], axis=1)


def _row_split_spec(tm):
    return pl.BlockSpec((tm * SUBLANES, LANES), lambda i: (i, 0))


def _in_proj_kernel(x_ref, g_ref, w_ref, *refs):
    out_refs, ybuf_ref = refs[:-1], refs[-1]
    h = _rms(x_ref[...], g_ref[...]).astype(BF16)
    for c, (epi, plain, views) in enumerate(_IN_LAYOUT):
        y = jnp.dot(h, w_ref[:, c * IN_CHUNK:(c + 1) * IN_CHUNK], preferred_element_type=F32)
        if epi == "gelu":
            y = _gelu(y)
        elif epi == "sigmoid":
            y = jax.nn.sigmoid(y)
        if plain is not None:
            slot, sub = plain
            out_refs[slot][:, sub * IN_CHUNK:(sub + 1) * IN_CHUNK] = y.astype(BF16)
        if any(d > 1 for _, d in views):
            for j in range(IN_CHUNK // LANES):
                ybuf_ref[j] = y[:, j * LANES:(j + 1) * LANES]
        for slot, d in views:
            ref = out_refs[slot]
            if d == 1:
                ref[...] = y.astype(BF16)
                continue
            for r in range(d):
                for j in range(IN_CHUNK // LANES):
                    c0 = r * IN_CHUNK + j * LANES
                    ref[:, c0:c0 + LANES] = ybuf_ref[j, pl.ds(r, ref.shape[0], stride=d), :].astype(BF16)


def _in_proj(x2d, g, w_bf16, tm=512):
    n = x2d.shape[0]
    width = w_bf16.shape[1]
    shapes = [(n, SGU_WIDTH), (n, SGU_WIDTH), (n, D_MODEL), (n, D_MODEL)]
    blocks = [(tm, SGU_WIDTH), (tm, SGU_WIDTH), (tm, D_MODEL), (tm, D_MODEL)]
    for d in ATT_DILATIONS:
        shapes += [(n // d, d * IN_CHUNK)] * 3
        blocks += [(tm // d, d * IN_CHUNK)] * 3
    return pl.pallas_call(
        _in_proj_kernel,
        grid=(n // tm,),
        in_specs=[pl.BlockSpec((tm, D_MODEL), lambda i: (i, 0)),
                  _full((1, D_MODEL)),
                  pl.BlockSpec((D_MODEL, width), lambda i: (0, 0), pipeline_mode=pl.Buffered(1))],
        out_specs=[pl.BlockSpec(blk, lambda i: (i, 0)) for blk in blocks],
        out_shape=[jax.ShapeDtypeStruct(shp, BF16) for shp in shapes],
        scratch_shapes=[pltpu.VMEM((IN_CHUNK // LANES, tm, LANES), F32)],
        compiler_params=_params("parallel"),
        name="in_proj",
    )(x2d, g, w_bf16)


def _sgu_kernel(u_ref, v_ref, ga_ref, ng_ref, ws_ref, bs_ref, wa_ref, out_ref):
    tm = u_ref.shape[0]
    vn = _rms(v_ref[...].astype(F32), ng_ref[...]).astype(BF16)
    chunks = []
    for c in range(tm // SGU_CHUNK):
        cols = []
        for g in range(SGU_GROUPS):
            blk = vn[c * SGU_CHUNK:(c + 1) * SGU_CHUNK, g * LANES:(g + 1) * LANES]
            cols.append(jnp.dot(ws_ref[g], blk, preferred_element_type=F32) + bs_ref[g])
        chunks.append(jnp.concatenate(cols, axis=1))
    mixed = jnp.concatenate(chunks, axis=0)
    ya = (u_ref[...].astype(F32) * mixed).astype(BF16)
    pa = jnp.dot(ya, wa_ref[...], preferred_element_type=F32) * ga_ref[...].astype(F32)
    out_ref[...] = pa.astype(BF16)


def _sgu(u, v, ga, ng, ws_bf16, bs_b, wa_bf16, tm=512):
    n = u.shape[0]
    row = pl.BlockSpec((tm, SGU_WIDTH), lambda i: (i, 0))
    return pl.pallas_call(
        _sgu_kernel,
        grid=(n // tm,),
        in_specs=[row, row, row, _full((1, SGU_WIDTH)),
                  _full((SGU_GROUPS, SGU_CHUNK, SGU_CHUNK)),
                  _full((SGU_GROUPS, SGU_CHUNK, LANES)),
                  _full((SGU_WIDTH, D_MODEL))],
        out_specs=pl.BlockSpec((tm, D_MODEL), lambda i: (i, 0)),
        out_shape=jax.ShapeDtypeStruct((n, D_MODEL), BF16),
        compiler_params=_params("parallel"),
        name="sgu",
    )(u, v, ga, ng, ws_bf16, bs_b, wa_bf16)


def _attn_kernel(q_ref, kp_ref, kc_ref, kn_ref, vp_ref, vc_ref, vn_ref, o_ref, lse_ref,
                 *, dilation, n_side, sub_len):
    tq = q_ref.shape[0]
    win = tq + 2 * n_side
    i = pl.program_id(2)
    k = jnp.concatenate([kp_ref[...], kc_ref[...], kn_ref[...]], axis=0)
    v = jnp.concatenate([vp_ref[...], vc_ref[...], vn_ref[...]], axis=0)
    row = lax.broadcasted_iota(I32, (tq, win), 0)
    col = lax.broadcasted_iota(I32, (tq, win), 1)
    rel = col - n_side - row
    pos = i * tq - n_side + col
    valid = (jnp.abs(rel) <= n_side) & (pos >= 0) & (pos < sub_len)
    negdist = -(jnp.abs(rel) * dilation).astype(F32)
    lane = lax.broadcasted_iota(I32, (tq, LANES), 1)
    low_half = lane < ATT_HEAD_DIM
    scale = np.float32(ATT_HEAD_DIM ** -0.5)
    for pair in range(ATT_HEADS // 2):
        sl = slice(pair * LANES, (pair + 1) * LANES)
        qp, kpair, vpair = q_ref[:, sl], k[:, sl], v[:, sl]
        outs, lses = [], []
        for j in range(2):
            slope = np.float32(2.0 ** (-8.0 * (pair * 2 + j + 1) / ATT_HEADS))
            qm = jnp.where(low_half if j == 0 else jnp.logical_not(low_half), qp, jnp.zeros_like(qp))
            s = lax.dot_general(qm, kpair, (((1,), (1,)), ((), ())), preferred_element_type=F32)
            s = jnp.where(valid, s * scale + slope * negdist, NEG_INF)
            m = jnp.max(s, axis=-1, keepdims=True)
            p = jnp.exp(s - m)
            l = jnp.sum(p, axis=-1, keepdims=True)
            o = jnp.dot(p.astype(BF16), vpair, preferred_element_type=F32)
            outs.append(o / l)
            lses.append(jnp.broadcast_to(m + jnp.log(l), (tq, LANES)))
        o_ref[:, sl] = jnp.where(low_half, outs[0], outs[1]).astype(o_ref.dtype)
        lse_ref[:, sl] = jnp.where(low_half, lses[0], lses[1])


ATT_QUERY_TILE = 128


def _attention_pattern(qv, kv, vv, batch, dilation, n_side):
    sub_len = qv.shape[0] // batch
    tq = min(ATT_QUERY_TILE, sub_len)
    kvw = ATT_KV_WIDTH
    b = batch
    qv, kv, vv = (t.reshape(b, sub_len, dilation * kvw) for t in (qv, kv, vv))
    per = tq // n_side
    last = sub_len // n_side - 1
    cur = pl.BlockSpec((None, tq, kvw), lambda bi, r, i: (bi, i, r))
    prev = pl.BlockSpec((None, n_side, kvw), lambda bi, r, i: (bi, jnp.maximum(i * per - 1, 0), r))
    nxt = pl.BlockSpec((None, n_side, kvw), lambda bi, r, i: (bi, jnp.minimum((i + 1) * per, last), r))
    o, lse = pl.pallas_call(
        functools.partial(_attn_kernel, dilation=dilation, n_side=n_side, sub_len=sub_len),
        grid=(b, dilation, sub_len // tq),
        in_specs=[cur, prev, cur, nxt, prev, cur, nxt],
        out_specs=[cur, cur],
        out_shape=[jax.ShapeDtypeStruct((b, sub_len, dilation * kvw), BF16),
                   jax.ShapeDtypeStruct((b, sub_len, dilation * kvw), F32)],
        compiler_params=_params("parallel", "parallel", "parallel"),
        name=f"attn_d{dilation}",
    )(qv, kv, kv, kv, vv, vv, vv)
    return o.reshape(b * sub_len, dilation * kvw), lse.reshape(b * sub_len, dilation * kvw)


def _from_residue_view(blk_ref, tmp_ref, d):
    if d == 1:
        return blk_ref[...].astype(F32)
    n_tiles = IN_CHUNK // LANES
    for r in range(d):
        for j in range(n_tiles):
            c0 = r * IN_CHUNK + j * LANES
            tmp_ref[j, pl.ds(r, blk_ref.shape[0], stride=d), :] = blk_ref[:, c0:c0 + LANES].astype(F32)
    return jnp.concatenate([tmp_ref[j] for j in range(n_tiles)], axis=1)


def _merge_kernel(x_ref, pa_ref, gb_ref, o1_ref, o2_ref, o3_ref, l1_ref, l2_ref, l3_ref,
                  wb_ref, wo_ref, g_ref, x1_ref, h2_ref, *tmp_refs):
    tmp = iter(tmp_refs)
    os_, ls_ = [], []
    for o_ref, l_ref, d in zip((o1_ref, o2_ref, o3_ref), (l1_ref, l2_ref, l3_ref), ATT_DILATIONS):
        os_.append(_from_residue_view(o_ref, next(tmp) if d > 1 else None, d))
        ls_.append(_from_residue_view(l_ref, next(tmp) if d > 1 else None, d))
    l1, l2, l3 = ls_
    m = jnp.maximum(jnp.maximum(l1, l2), l3)
    w1, w2, w3 = jnp.exp(l1 - m), jnp.exp(l2 - m), jnp.exp(l3 - m)
    yb = (w1 * os_[0] + w2 * os_[1] + w3 * os_[2]) / (w1 + w2 + w3)
    pb = jnp.dot(yb.astype(BF16), wb_ref[...], preferred_element_type=F32) * gb_ref[...].astype(F32)
    merged = (pa_ref[...].astype(F32) + pb).astype(BF16)
    x1 = x_ref[...] + jnp.dot(merged, wo_ref[...], preferred_element_type=F32)
    _store_row_split(x1_ref, x1)
    _store_row_split(h2_ref, _rms(x1, g_ref[...]))


def _merge(x2d, pa, gb, os_, ls_, wb_bf16, wo_bf16, g, tm=512):
    n = x2d.shape[0]
    wide = pl.BlockSpec((tm, D_MODEL), lambda i: (i, 0))
    views = [pl.BlockSpec((tm // d, d * ATT_KV_WIDTH), lambda i: (i, 0)) for d in ATT_DILATIONS]
    n_tmp = 2 * sum(d > 1 for d in ATT_DILATIONS)
    return pl.pallas_call(
        _merge_kernel,
        grid=(n // tm,),
        in_specs=[wide, wide, wide, *views, *views,
                  _full((ATT_KV_WIDTH, D_MODEL)), _full((D_MODEL, D_MODEL)), _full((1, D_MODEL))],
        out_specs=[_row_split_spec(tm), _row_split_spec(tm)],
        out_shape=[jax.ShapeDtypeStruct((n * SUBLANES, LANES), F32)] * 2,
        scratch_shapes=[pltpu.VMEM((ATT_KV_WIDTH // LANES, tm, LANES), F32)] * n_tmp,
        compiler_params=_params("parallel"),
        name="merge",
    )(x2d, pa, gb, *os_, *ls_, wb_bf16, wo_bf16, g)


def _topk_rows(s, k, payload=None):
    rows = lax.broadcasted_iota(I32, s.shape, 0).astype(F32)
    vals, ids = [], []
    for _ in range(k):
        m = jnp.max(s, axis=0, keepdims=True)
        am = jnp.min(jnp.where(s == m, rows, np.float32(s.shape[0])), axis=0, keepdims=True)
        hit = rows == am
        vals.append(m)
        if payload is None:
            ids.append(am)
        else:
            ids.append(jnp.max(jnp.where(hit, payload, -1.0), axis=0, keepdims=True))
        s = jnp.where(hit, -jnp.inf, s)
    return jnp.concatenate(vals, axis=0), jnp.concatenate(ids, axis=0)


def _pair_candidates(s0, s1, i0, i1):
    vals, ids = [], []
    for a in range(SUBLANES):
        n_b = PEER_TOPK // (a + 1)
        rows = PEER_TOPK if n_b > SUBLANES else SUBLANES
        v = s0[a:a + 1] + s1[0:rows]
        if n_b < rows:
            v = jnp.where(lax.broadcasted_iota(I32, v.shape, 0) < n_b, v, -jnp.inf)
        vals.append(v)
        ids.append(i0[a:a + 1] * PEER_N_KEYS + i1[0:rows])
    vals.append(s0[SUBLANES:PEER_TOPK] + s1[0:1])
    ids.append(i0[SUBLANES:PEER_TOPK] * PEER_N_KEYS + i1[0:1])
    return jnp.concatenate(vals, axis=0), jnp.concatenate(ids, axis=0)


def _split_bf16(a):
    hi = a.astype(BF16)
    return hi, (a - hi.astype(F32)).astype(BF16)


def _peer_topk_kernel(h_ref, wq_ref, skh_ref, skl_ref, idx_ref, gate_ref):
    q = jnp.dot(_load_row_split(h_ref).astype(BF16), wq_ref[...], preferred_element_type=F32)
    nt = (((1,), (1,)), ((), ()))
    idx_rows, gate_rows = [], []
    for hd in range(PEER_HEADS):
        top_s, top_i = [], []
        for p in range(2):
            hp = hd * 2 + p
            q_hi, q_lo = _split_bf16(q[:, hp * PEER_HALF:(hp + 1) * PEER_HALF])
            kh, kl = skh_ref[hp], skl_ref[hp]
            st = (lax.dot_general(kh, q_hi, nt, preferred_element_type=F32)
                  + lax.dot_general(kh, q_lo, nt, preferred_element_type=F32)
                  + lax.dot_general(kl, q_hi, nt, preferred_element_type=F32))
            ts, ti = _topk_rows(st, PEER_TOPK)
            top_s.append(ts)
            top_i.append(ti)
        cand, cidx = _pair_candidates(top_s[0], top_s[1], top_i[0], top_i[1])
        best_s, expert = _topk_rows(cand, PEER_TOPK, payload=cidx)
        e = jnp.exp(best_s - best_s[0:1])
        gate_rows.append(e / jnp.sum(e, axis=0, keepdims=True))
        idx_rows.append(expert * np.float32(ROW_SUBLANES))
    idx_ref[...] = jnp.concatenate(idx_rows, axis=0).T.astype(I32)
    gate_ref[...] = jnp.concatenate(gate_rows, axis=0).T


def _peer_topk(h2_split, wq_bf16, sk_hi, sk_lo, tm=256):
    n = h2_split.shape[0] // SUBLANES
    qw = wq_bf16.shape[1]
    out = pl.BlockSpec((tm, PEER_PICKS), lambda i: (i, 0))
    return pl.pallas_call(
        _peer_topk_kernel,
        grid=(n // tm,),
        in_specs=[_row_split_spec(tm),
                  _full((D_MODEL, qw)),
                  _full((2 * PEER_HEADS, PEER_N_KEYS, PEER_HALF)),
                  _full((2 * PEER_HEADS, PEER_N_KEYS, PEER_HALF))],
        out_specs=[out, out],
        out_shape=[jax.ShapeDtypeStruct((n, PEER_PICKS), I32), jax.ShapeDtypeStruct((n, PEER_PICKS), F32)],
        compiler_params=_params("parallel"),
        name="peer_topk",
    )(h2_split, wq_bf16, sk_hi, sk_lo)


def _pack_table(tab):
    lo16 = lax.bitcast_convert_type(tab[:, :ROW_WORDS].astype(BF16), jnp.uint16).astype(jnp.uint32)
    full = lax.bitcast_convert_type(tab[:, ROW_WORDS:], jnp.uint32)
    sign = full & jnp.uint32(0x80000000)
    mag = (full & jnp.uint32(0x7FFFFFFF)) + jnp.uint32(0x8000)
    hi16 = jnp.where(mag >= lo16, (mag - lo16) >> 16, jnp.uint32(0))
    word = sign | (hi16 << 16) | lo16
    return lax.bitcast_convert_type(word, I32).reshape(tab.shape[0] * ROW_SUBLANES, LANES)


def _gather_row(tab_ref, e4):
    w = tab_ref[pl.ds(pl.multiple_of(e4, ROW_SUBLANES), ROW_SUBLANES), :]
    return lax.bitcast_convert_type(w << 16, F32), lax.bitcast_convert_type(w, F32)


def _lane_sums(rows):
    return jnp.dot(rows.astype(BF16), jnp.ones((LANES, LANES), BF16), preferred_element_type=F32)


def _pick_diag():
    return (lax.broadcasted_iota(I32, (PEER_PICKS, LANES), 0)
            == lax.broadcasted_iota(I32, (PEER_PICKS, LANES), 1))


STAGE_TOKENS = 16
TOKENS_PER_GROUP = 2 * STAGE_TOKENS


def _stage_copy(idx_ref, stage, bufs, sems, b):
    rows = pl.ds(pl.multiple_of(stage * STAGE_TOKENS, STAGE_TOKENS), STAGE_TOKENS)
    return pltpu.make_async_copy(idx_ref.at[rows], bufs[b], sems.at[b])


def _for_group_tokens(g, n_stages, idx_ref, bufs, sems, token_fn):
    for b in range(2):
        stage = 2 * g + b
        _stage_copy(idx_ref, stage, bufs, sems, b).wait()
        for tt in range(STAGE_TOKENS):
            local = b * STAGE_TOKENS + tt
            token_fn(local, g * TOKENS_PER_GROUP + local, lambda k, b=b, tt=tt: bufs[b][tt, k])

        @pl.when(stage + 2 < n_stages)
        def _():
            _stage_copy(idx_ref, stage + 2, bufs, sems, b).start()


def _start_first_stages(idx_ref, bufs, sems):
    for b in range(2):
        _stage_copy(idx_ref, b, bufs, sems, b).start()


def _peer_u_kernel(idx_ref, h_ref, gate_ref, tab_ref, act_ref, slot_ref, stage_a, stage_b, sems):
    tm = gate_ref.shape[0]
    slot_rows = PEER_PICKS * ROW_SUBLANES
    diag = _pick_diag()
    bufs = (stage_a, stage_b)
    _start_first_stages(idx_ref, bufs, sems)

    def token(local, t, pick_offset):
        hv = h_ref[pl.ds(pl.multiple_of(t * SUBLANES, SUBLANES), SUBLANES), :]
        h_lo, h_hi = hv[0:ROW_SUBLANES], hv[ROW_SUBLANES:SUBLANES]
        for k in range(PEER_PICKS):
            lo, hi = _gather_row(tab_ref, pick_offset(k))
            r0 = local * slot_rows + k * ROW_SUBLANES
            slot_ref[r0:r0 + ROW_SUBLANES, :] = lo * h_lo + hi * h_hi

    def group(g, carry):
        _for_group_tokens(g, tm // STAGE_TOKENS, idx_ref, bufs, sems, token)
        rows = []
        for tt in range(TOKENS_PER_GROUP):
            r = slot_ref[pl.ds(tt * slot_rows, PEER_PICKS, stride=ROW_SUBLANES), :]
            for s in range(1, ROW_SUBLANES):
                r = r + slot_ref[pl.ds(tt * slot_rows + s, PEER_PICKS, stride=ROW_SUBLANES), :]
            rows.append(jnp.sum(jnp.where(diag, _lane_sums(r), 0.0), axis=0, keepdims=True))
        off = pl.multiple_of(g * TOKENS_PER_GROUP, TOKENS_PER_GROUP)
        a = jnp.concatenate(rows, axis=0)
        act_ref[pl.ds(off, TOKENS_PER_GROUP), :] = _gelu(a) * gate_ref[pl.ds(off, TOKENS_PER_GROUP), :]
        return carry

    lax.fori_loop(0, tm // TOKENS_PER_GROUP, group, 0)


def _stage_scratch():
    stage = pltpu.SMEM((STAGE_TOKENS, PEER_PICKS), I32)
    return [stage, stage, pltpu.SemaphoreType.DMA((2,))]


def _peer_u(idx, h2_split, gate, tab, tm=512):
    n = idx.shape[0]
    tm = min(tm, n)
    picks = pl.BlockSpec((tm, PEER_PICKS), lambda i: (i, 0))
    return pl.pallas_call(
        _peer_u_kernel,
        grid=(n // tm,),
        in_specs=[picks, _row_split_spec(tm), picks, pl.BlockSpec(memory_space=pltpu.VMEM)],
        out_specs=picks,
        out_shape=jax.ShapeDtypeStruct((n, PEER_PICKS), F32),
        scratch_shapes=[pltpu.VMEM((TOKENS_PER_GROUP * PEER_PICKS * ROW_SUBLANES, LANES), F32), *_stage_scratch()],
        compiler_params=_params("arbitrary"),
        name="peer_u",
    )(idx, h2_split, gate, tab)


N_ACC = 4


def _peer_v_kernel(idx_ref, act_ref, x_ref, g_ref, tab_ref, out_ref, actb_ref, x2_ref, stage_a, stage_b, sems,
                   *, normalize):
    tm = idx_ref.shape[0]
    diag = _pick_diag()
    bufs = (stage_a, stage_b)
    _start_first_stages(idx_ref, bufs, sems)

    def token(local, t, pick_offset):
        acc_lo = [jnp.zeros((ROW_SUBLANES, LANES), F32) for _ in range(N_ACC)]
        acc_hi = [jnp.zeros((ROW_SUBLANES, LANES), F32) for _ in range(N_ACC)]
        for k in range(PEER_PICKS):
            lo, hi = _gather_row(tab_ref, pick_offset(k))
            a = actb_ref[local * PEER_PICKS + k:local * PEER_PICKS + k + 1, :]
            acc_lo[k % N_ACC] = acc_lo[k % N_ACC] + a * lo
            acc_hi[k % N_ACC] = acc_hi[k % N_ACC] + a * hi
        lo = (acc_lo[0] + acc_lo[1]) + (acc_lo[2] + acc_lo[3])
        hi = (acc_hi[0] + acc_hi[1]) + (acc_hi[2] + acc_hi[3])
        r0 = pl.multiple_of(t * SUBLANES, SUBLANES)
        r1 = pl.multiple_of(t * SUBLANES + ROW_SUBLANES, ROW_SUBLANES)
        x2_ref[pl.ds(r0, ROW_SUBLANES), :] = x_ref[pl.ds(r0, ROW_SUBLANES), :] + lo
        x2_ref[pl.ds(r1, ROW_SUBLANES), :] = x_ref[pl.ds(r1, ROW_SUBLANES), :] + hi

    def group(g, carry):
        off = pl.multiple_of(g * TOKENS_PER_GROUP, TOKENS_PER_GROUP)
        for tt in range(TOKENS_PER_GROUP):
            spread = jnp.where(diag, act_ref[pl.ds(off + tt, 1), :], 0.0)
            actb_ref[tt * PEER_PICKS:(tt + 1) * PEER_PICKS, :] = _lane_sums(spread)
        _for_group_tokens(g, tm // STAGE_TOKENS, idx_ref, bufs, sems, token)
        return carry

    lax.fori_loop(0, tm // TOKENS_PER_GROUP, group, 0)
    x2 = _load_row_split(x2_ref)
    out_ref[...] = _rms(x2, g_ref[...]) if normalize else x2


def _peer_v(idx, act, x1_split, tab, final_g, tm=512):
    n = idx.shape[0]
    tm = min(tm, n)
    normalize = final_g is not None
    g = final_g if normalize else jnp.ones((1, D_MODEL), F32)
    picks = pl.BlockSpec((tm, PEER_PICKS), lambda i: (i, 0))
    return pl.pallas_call(
        functools.partial(_peer_v_kernel, normalize=normalize),
        grid=(n // tm,),
        in_specs=[picks, picks, _row_split_spec(tm), _full((1, D_MODEL)), pl.BlockSpec(memory_space=pltpu.VMEM)],
        out_specs=pl.BlockSpec((tm, D_MODEL), lambda i: (i, 0)),
        out_shape=jax.ShapeDtypeStruct((n, D_MODEL), F32),
        scratch_shapes=[pltpu.VMEM((TOKENS_PER_GROUP * PEER_PICKS, LANES), F32),
                        pltpu.VMEM((tm * SUBLANES, LANES), F32), *_stage_scratch()],
        compiler_params=_params("arbitrary"),
        name="peer_v",
    )(idx, act, x1_split, g, tab)


def kernel(x, norm_mix_g, w_in, sgu_norm_g, sgu_w, sgu_b, w_branch_a, w_branch_b, w_out,
           norm_ffn_g, peer_wq, peer_subkeys, peer_u, peer_v, norm_final_g):
    b, s, d = x.shape
    n = b * s
    x2 = x.reshape(n, d)
    for l in range(w_in.shape[0]):
        u, v, ga, gb, *qkv_views = _in_proj(x2, norm_mix_g[l].reshape(1, d), w_in[l].astype(BF16))
        bs_b = jnp.broadcast_to(sgu_b[l][:, :, None], (SGU_GROUPS, SGU_CHUNK, LANES))
        pa = _sgu(u, v, ga, sgu_norm_g[l].reshape(1, SGU_WIDTH), sgu_w[l].astype(BF16), bs_b,
                  w_branch_a[l].astype(BF16))
        os_, ls_ = [], []
        for g, (window, dilation) in enumerate(ATT_PATTERNS):
            qv, kv, vv = qkv_views[3 * g:3 * g + 3]
            o, lse = _attention_pattern(qv, kv, vv, b, dilation, window // (2 * dilation))
            os_.append(o)
            ls_.append(lse)
        x1, h2 = _merge(x2, pa, gb, os_, ls_, w_branch_b[l].astype(BF16), w_out[l].astype(BF16),
                        norm_ffn_g[l].reshape(1, d))
        sk = peer_subkeys[l].reshape(2 * PEER_HEADS, PEER_N_KEYS, PEER_HALF)
        sk_hi = sk.astype(BF16)
        sk_lo = (sk - sk_hi.astype(F32)).astype(BF16)
        idx, gate = _peer_topk(h2, peer_wq[l].astype(BF16), sk_hi, sk_lo)
        act = _peer_u(idx, h2, gate, _pack_table(peer_u[l]))
        last = l == w_in.shape[0] - 1
        x2 = _peer_v(idx, act, x1, _pack_table(peer_v[l]), norm_final_g.reshape(1, d) if last else None)
    return x2.reshape(b, s, d)
```

```python
import functools

import numpy as np
import jax
import jax.numpy as jnp
from jax import lax
from jax.experimental import pallas as pl
from jax.experimental.pallas import tpu as pltpu

F32 = jnp.float32
BF16 = jnp.bfloat16
I32 = jnp.int32

D_MODEL = 1024
EPS = 1e-6
NEG_INF = -1e30
SGU_CHUNK = 128
SGU_GROUPS = 8
SGU_WIDTH = 1024
ATT_HEADS = 8
ATT_HEAD_DIM = 64
ATT_PATTERNS = ((128, 1), (512, 4), (2048, 16))
ATT_GROUPS = 3
ATT_KV_WIDTH = ATT_HEADS * ATT_HEAD_DIM
ATT_Q_WIDTH = ATT_GROUPS * ATT_KV_WIDTH
PEER_HEADS = 8
PEER_N_KEYS = 128
PEER_N_EXPERTS = PEER_N_KEYS * PEER_N_KEYS
PEER_HALF = 128
PEER_TOPK = 16
PEER_PICKS = PEER_HEADS * PEER_TOPK

LANES = 128
SUBLANES = 8
ROW_WORDS = D_MODEL // 2
ROW_SUBLANES = ROW_WORDS // LANES
VMEM_LIMIT = 56 * 1024 * 1024

IN_CHUNK = 512
ATT_DILATIONS = tuple(d for _, d in ATT_PATTERNS)
_N_PLAIN = 4
_IN_LAYOUT = (
    ("gelu", (0, 0), ()), ("gelu", (0, 1), ()), ("gelu", (1, 0), ()), ("gelu", (1, 1), ()),
    *((None, None, ((_N_PLAIN + 3 * g, d),)) for g, d in enumerate(ATT_DILATIONS)),
    (None, None, tuple((_N_PLAIN + 3 * g + 1, d) for g, d in enumerate(ATT_DILATIONS))),
    (None, None, tuple((_N_PLAIN + 3 * g + 2, d) for g, d in enumerate(ATT_DILATIONS))),
    ("sigmoid", (2, 0), ()), ("sigmoid", (2, 1), ()), ("sigmoid", (3, 0), ()), ("sigmoid", (3, 1), ()),
)


def _params(*sem):
    return pltpu.CompilerParams(dimension_semantics=sem, vmem_limit_bytes=VMEM_LIMIT)


def _gelu(y):
    return 0.5 * y * (1.0 + lax.erf(y * np.float32(1.0 / np.sqrt(2.0))))


def _rms(x, g):
    return x * lax.rsqrt(jnp.mean(x * x, axis=-1, keepdims=True) + EPS) * g


def _full(shape):
    return pl.BlockSpec(shape, lambda *_: (0,) * len(shape))


def _store_row_split(ref, x):
    tm = x.shape[0]
    for s in range(SUBLANES):
        ref[pl.ds(s, tm, stride=SUBLANES), :] = x[:, s * LANES:(s + 1) * LANES]


def _load_row_split(ref):
    tm = ref.shape[0] // SUBLANES
    return jnp.concatenate([ref[pl.ds(s, tm, stride=SUBLANES), :] for s in range(SUBLANES)], axis=1)


def _row_split_spec(tm):
    return pl.BlockSpec((tm * SUBLANES, LANES), lambda i: (i, 0))


def _in_proj_kernel(x_ref, g_ref, w_ref, *refs):
    out_refs, ybuf_ref = refs[:-1], refs[-1]
    h = _rms(x_ref[...], g_ref[...]).astype(BF16)
    for c, (epi, plain, views) in enumerate(_IN_LAYOUT):
        y = jnp.dot(h, w_ref[:, c * IN_CHUNK:(c + 1) * IN_CHUNK], preferred_element_type=F32)
        if epi == "gelu":
            y = _gelu(y)
        elif epi == "sigmoid":
            y = jax.nn.sigmoid(y)
        if plain is not None:
            slot, sub = plain
            out_refs[slot][:, sub * IN_CHUNK:(sub + 1) * IN_CHUNK] = y.astype(BF16)
        if any(d > 1 for _, d in views):
            for j in range(IN_CHUNK // LANES):
                ybuf_ref[j] = y[:, j * LANES:(j + 1) * LANES]
        for slot, d in views:
            ref = out_refs[slot]
            if d == 1:
                ref[...] = y.astype(BF16)
                continue
            for r in range(d):
                for j in range(IN_CHUNK // LANES):
                    c0 = r * IN_CHUNK + j * LANES
                    ref[:, c0:c0 + LANES] = ybuf_ref[j, pl.ds(r, ref.shape[0], stride=d), :].astype(BF16)


def _in_proj(x2d, g, w_bf16, tm=512):
    n = x2d.shape[0]
    width = w_bf16.shape[1]
    shapes = [(n, SGU_WIDTH), (n, SGU_WIDTH), (n, D_MODEL), (n, D_MODEL)]
    blocks = [(tm, SGU_WIDTH), (tm, SGU_WIDTH), (tm, D_MODEL), (tm, D_MODEL)]
    for d in ATT_DILATIONS:
        shapes += [(n // d, d * IN_CHUNK)] * 3
        blocks += [(tm // d, d * IN_CHUNK)] * 3
    return pl.pallas_call(
        _in_proj_kernel,
        grid=(n // tm,),
        in_specs=[pl.BlockSpec((tm, D_MODEL), lambda i: (i, 0)),
                  _full((1, D_MODEL)),
                  pl.BlockSpec((D_MODEL, width), lambda i: (0, 0), pipeline_mode=pl.Buffered(1))],
        out_specs=[pl.BlockSpec(blk, lambda i: (i, 0)) for blk in blocks],
        out_shape=[jax.ShapeDtypeStruct(shp, BF16) for shp in shapes],
        scratch_shapes=[pltpu.VMEM((IN_CHUNK // LANES, tm, LANES), F32)],
        compiler_params=_params("parallel"),
        name="in_proj",
    )(x2d, g, w_bf16)


def _sgu_kernel(u_ref, v_ref, ga_ref, ng_ref, ws_ref, bs_ref, wa_ref, out_ref):
    tm = u_ref.shape[0]
    vn = _rms(v_ref[...].astype(F32), ng_ref[...]).astype(BF16)
    chunks = []
    for c in range(tm // SGU_CHUNK):
        cols = []
        for g in range(SGU_GROUPS):
            blk = vn[c * SGU_CHUNK:(c + 1) * SGU_CHUNK, g * LANES:(g + 1) * LANES]
            cols.append(jnp.dot(ws_ref[g], blk, preferred_element_type=F32) + bs_ref[g])
        chunks.append(jnp.concatenate(cols, axis=1))
    mixed = jnp.concatenate(chunks, axis=0)
    ya = (u_ref[...].astype(F32) * mixed).astype(BF16)
    pa = jnp.dot(ya, wa_ref[...], preferred_element_type=F32) * ga_ref[...].astype(F32)
    out_ref[...] = pa.astype(BF16)


def _sgu(u, v, ga, ng, ws_bf16, bs_b, wa_bf16, tm=512):
    n = u.shape[0]
    row = pl.BlockSpec((tm, SGU_WIDTH), lambda i: (i, 0))
    return pl.pallas_call(
        _sgu_kernel,
        grid=(n // tm,),
        in_specs=[row, row, row, _full((1, SGU_WIDTH)),
                  _full((SGU_GROUPS, SGU_CHUNK, SGU_CHUNK)),
                  _full((SGU_GROUPS, SGU_CHUNK, LANES)),
                  _full((SGU_WIDTH, D_MODEL))],
        out_specs=pl.BlockSpec((tm, D_MODEL), lambda i: (i, 0)),
        out_shape=jax.ShapeDtypeStruct((n, D_MODEL), BF16),
        compiler_params=_params("parallel"),
        name="sgu",
    )(u, v, ga, ng, ws_bf16, bs_b, wa_bf16)


def _attn_kernel(q_ref, kp_ref, kc_ref, kn_ref, vp_ref, vc_ref, vn_ref, o_ref, lse_ref,
                 *, dilation, n_side, sub_len):
    tq = q_ref.shape[0]
    win = tq + 2 * n_side
    i = pl.program_id(2)
    k = jnp.concatenate([kp_ref[...], kc_ref[...], kn_ref[...]], axis=0)
    v = jnp.concatenate([vp_ref[...], vc_ref[...], vn_ref[...]], axis=0)
    row = lax.broadcasted_iota(I32, (tq, win), 0)
    col = lax.broadcasted_iota(I32, (tq, win), 1)
    rel = col - n_side - row
    pos = i * tq - n_side + col
    valid = (jnp.abs(rel) <= n_side) & (pos >= 0) & (pos < sub_len)
    negdist = -(jnp.abs(rel) * dilation).astype(F32)
    lane = lax.broadcasted_iota(I32, (tq, LANES), 1)
    low_half = lane < ATT_HEAD_DIM
    scale = np.float32(ATT_HEAD_DIM ** -0.5)
    for pair in range(ATT_HEADS // 2):
        sl = slice(pair * LANES, (pair + 1) * LANES)
        qp, kpair, vpair = q_ref[:, sl], k[:, sl], v[:, sl]
        outs, lses = [], []
        for j in range(2):
            slope = np.float32(2.0 ** (-8.0 * (pair * 2 + j + 1) / ATT_HEADS))
            qm = jnp.where(low_half if j == 0 else jnp.logical_not(low_half), qp, jnp.zeros_like(qp))
            s = lax.dot_general(qm, kpair, (((1,), (1,)), ((), ())), preferred_element_type=F32)
            s = jnp.where(valid, s * scale + slope * negdist, NEG_INF)
            m = jnp.max(s, axis=-1, keepdims=True)
            p = jnp.exp(s - m)
            l = jnp.sum(p, axis=-1, keepdims=True)
            o = jnp.dot(p.astype(BF16), vpair, preferred_element_type=F32)
            outs.append(o / l)
            lses.append(jnp.broadcast_to(m + jnp.log(l), (tq, LANES)))
        o_ref[:, sl] = jnp.where(low_half, outs[0], outs[1]).astype(o_ref.dtype)
        lse_ref[:, sl] = jnp.where(low_half, lses[0], lses[1])


ATT_QUERY_TILE = 128


def _attention_pattern(qv, kv, vv, batch, dilation, n_side):
    sub_len = qv.shape[0] // batch
    tq = min(ATT_QUERY_TILE, sub_len)
    kvw = ATT_KV_WIDTH
    b = batch
    qv, kv, vv = (t.reshape(b, sub_len, dilation * kvw) for t in (qv, kv, vv))
    per = tq // n_side
    last = sub_len // n_side - 1
    cur = pl.BlockSpec((None, tq, kvw), lambda bi, r, i: (bi, i, r))
    prev = pl.BlockSpec((None, n_side, kvw), lambda bi, r, i: (bi, jnp.maximum(i * per - 1, 0), r))
    nxt = pl.BlockSpec((None, n_side, kvw), lambda bi, r, i: (bi, jnp.minimum((i + 1) * per, last), r))
    o, lse = pl.pallas_call(
        functools.partial(_attn_kernel, dilation=dilation, n_side=n_side, sub_len=sub_len),
        grid=(b, dilation, sub_len // tq),
        in_specs=[cur, prev, cur, nxt, prev, cur, nxt],
        out_specs=[cur, cur],
        out_shape=[jax.ShapeDtypeStruct((b, sub_len, dilation * kvw), BF16),
                   jax.ShapeDtypeStruct((b, sub_len, dilation * kvw), F32)],
        compiler_params=_params("parallel", "parallel", "parallel"),
        name=f"attn_d{dilation}",
    )(qv, kv, kv, kv, vv, vv, vv)
    return o.reshape(b * sub_len, dilation * kvw), lse.reshape(b * sub_len, dilation * kvw)


def _from_residue_view(blk_ref, tmp_ref, d):
    if d == 1:
        return blk_ref[...].astype(F32)
    n_tiles = IN_CHUNK // LANES
    for r in range(d):
        for j in range(n_tiles):
            c0 = r * IN_CHUNK + j * LANES
            tmp_ref[j, pl.ds(r, blk_ref.shape[0], stride=d), :] = blk_ref[:, c0:c0 + LANES].astype(F32)
    return jnp.concatenate([tmp_ref[j] for j in range(n_tiles)], axis=1)


def _merge_kernel(x_ref, pa_ref, gb_ref, o1_ref, o2_ref, o3_ref, l1_ref, l2_ref, l3_ref,
                  wb_ref, wo_ref, g_ref, x1_ref, h2_ref, *tmp_refs):
    tmp = iter(tmp_refs)
    os_, ls_ = [], []
    for o_ref, l_ref, d in zip((o1_ref, o2_ref, o3_ref), (l1_ref, l2_ref, l3_ref), ATT_DILATIONS):
        os_.append(_from_residue_view(o_ref, next(tmp) if d > 1 else None, d))
        ls_.append(_from_residue_view(l_ref, next(tmp) if d > 1 else None, d))
    l1, l2, l3 = ls_
    m = jnp.maximum(jnp.maximum(l1, l2), l3)
    w1, w2, w3 = jnp.exp(l1 - m), jnp.exp(l2 - m), jnp.exp(l3 - m)
    yb = (w1 * os_[0] + w2 * os_[1] + w3 * os_[2]) / (w1 + w2 + w3)
    pb = jnp.dot(yb.astype(BF16), wb_ref[...], preferred_element_type=F32) * gb_ref[...].astype(F32)
    merged = (pa_ref[...].astype(F32) + pb).astype(BF16)
    x1 = x_ref[...] + jnp.dot(merged, wo_ref[...], preferred_element_type=F32)
    _store_row_split(x1_ref, x1)
    _store_row_split(h2_ref, _rms(x1, g_ref[...]))


def _merge(x2d, pa, gb, os_, ls_, wb_bf16, wo_bf16, g, tm=512):
    n = x2d.shape[0]
    wide = pl.BlockSpec((tm, D_MODEL), lambda i: (i, 0))
    views = [pl.BlockSpec((tm // d, d * ATT_KV_WIDTH), lambda i: (i, 0)) for d in ATT_DILATIONS]
    n_tmp = 2 * sum(d > 1 for d in ATT_DILATIONS)
    return pl.pallas_call(
        _merge_kernel,
        grid=(n // tm,),
        in_specs=[wide, wide, wide, *views, *views,
                  _full((ATT_KV_WIDTH, D_MODEL)), _full((D_MODEL, D_MODEL)), _full((1, D_MODEL))],
        out_specs=[_row_split_spec(tm), _row_split_spec(tm)],
        out_shape=[jax.ShapeDtypeStruct((n * SUBLANES, LANES), F32)] * 2,
        scratch_shapes=[pltpu.VMEM((ATT_KV_WIDTH // LANES, tm, LANES), F32)] * n_tmp,
        compiler_params=_params("parallel"),
        name="merge",
    )(x2d, pa, gb, *os_, *ls_, wb_bf16, wo_bf16, g)


def _topk_rows(s, k, payload=None):
    rows = lax.broadcasted_iota(I32, s.shape, 0).astype(F32)
    vals, ids = [], []
    for _ in range(k):
        m = jnp.max(s, axis=0, keepdims=True)
        am = jnp.min(jnp.where(s == m, rows, np.float32(s.shape[0])), axis=0, keepdims=True)
        hit = rows == am
        vals.append(m)
        if payload is None:
            ids.append(am)
        else:
            ids.append(jnp.max(jnp.where(hit, payload, -1.0), axis=0, keepdims=True))
        s = jnp.where(hit, -jnp.inf, s)
    return jnp.concatenate(vals, axis=0), jnp.concatenate(ids, axis=0)


def _pair_candidates(s0, s1, i0, i1):
    vals, ids = [], []
    for a in range(SUBLANES):
        n_b = PEER_TOPK // (a + 1)
        rows = PEER_TOPK if n_b > SUBLANES else SUBLANES
        v = s0[a:a + 1] + s1[0:rows]
        if n_b < rows:
            v = jnp.where(lax.broadcasted_iota(I32, v.shape, 0) < n_b, v, -jnp.inf)
        vals.append(v)
        ids.append(i0[a:a + 1] * PEER_N_KEYS + i1[0:rows])
    vals.append(s0[SUBLANES:PEER_TOPK] + s1[0:1])
    ids.append(i0[SUBLANES:PEER_TOPK] * PEER_N_KEYS + i1[0:1])
    return jnp.concatenate(vals, axis=0), jnp.concatenate(ids, axis=0)


def _split_bf16(a):
    hi = a.astype(BF16)
    return hi, (a - hi.astype(F32)).astype(BF16)


def _peer_topk_kernel(h_ref, wq_ref, skh_ref, skl_ref, idx_ref, gate_ref):
    q = jnp.dot(_load_row_split(h_ref).astype(BF16), wq_ref[...], preferred_element_type=F32)
    nt = (((1,), (1,)), ((), ()))
    idx_rows, gate_rows = [], []
    for hd in range(PEER_HEADS):
        top_s, top_i = [], []
        for p in range(2):
            hp = hd * 2 + p
            q_hi, q_lo = _split_bf16(q[:, hp * PEER_HALF:(hp + 1) * PEER_HALF])
            kh, kl = skh_ref[hp], skl_ref[hp]
            st = (lax.dot_general(kh, q_hi, nt, preferred_element_type=F32)
                  + lax.dot_general(kh, q_lo, nt, preferred_element_type=F32)
                  + lax.dot_general(kl, q_hi, nt, preferred_element_type=F32))
            ts, ti = _topk_rows(st, PEER_TOPK)
            top_s.append(ts)
            top_i.append(ti)
        cand, cidx = _pair_candidates(top_s[0], top_s[1], top_i[0], top_i[1])
        best_s, expert = _topk_rows(cand, PEER_TOPK, payload=cidx)
        e = jnp.exp(best_s - best_s[0:1])
        gate_rows.append(e / jnp.sum(e, axis=0, keepdims=True))
        idx_rows.append(expert * np.float32(ROW_SUBLANES))
    idx_ref[...] = jnp.concatenate(idx_rows, axis=0).T.astype(I32)
    gate_ref[...] = jnp.concatenate(gate_rows, axis=0).T


def _peer_topk(h2_split, wq_bf16, sk_hi, sk_lo, tm=256):
    n = h2_split.shape[0] // SUBLANES
    qw = wq_bf16.shape[1]
    out = pl.BlockSpec((tm, PEER_PICKS), lambda i: (i, 0))
    return pl.pallas_call(
        _peer_topk_kernel,
        grid=(n // tm,),
        in_specs=[_row_split_spec(tm),
                  _full((D_MODEL, qw)),
                  _full((2 * PEER_HEADS, PEER_N_KEYS, PEER_HALF)),
                  _full((2 * PEER_HEADS, PEER_N_KEYS, PEER_HALF))],
        out_specs=[out, out],
        out_shape=[jax.ShapeDtypeStruct((n, PEER_PICKS), I32), jax.ShapeDtypeStruct((n, PEER_PICKS), F32)],
        compiler_params=_params("parallel"),
        name="peer_topk",
    )(h2_split, wq_bf16, sk_hi, sk_lo)


def _pack_table(tab):
    lo16 = lax.bitcast_convert_type(tab[:, :ROW_WORDS].astype(BF16), jnp.uint16).astype(jnp.uint32)
    full = lax.bitcast_convert_type(tab[:, ROW_WORDS:], jnp.uint32)
    sign = full & jnp.uint32(0x80000000)
    mag = (full & jnp.uint32(0x7FFFFFFF)) + jnp.uint32(0x8000)
    hi16 = jnp.where(mag >= lo16, (mag - lo16) >> 16, jnp.uint32(0))
    word = sign | (hi16 << 16) | lo16
    return lax.bitcast_convert_type(word, I32).reshape(tab.shape[0] * ROW_SUBLANES, LANES)


def _gather_words(tab_ref, e4):
    return tab_ref[pl.ds(pl.multiple_of(e4, ROW_SUBLANES), ROW_SUBLANES), :]


def _unpack_words(w):
    return lax.bitcast_convert_type(w << 16, F32), lax.bitcast_convert_type(w, F32)


def _gather_row(tab_ref, e4):
    return _unpack_words(_gather_words(tab_ref, e4))


def _lane_sums(rows):
    return jnp.dot(rows.astype(BF16), jnp.ones((LANES, LANES), BF16), preferred_element_type=F32)


def _pick_diag():
    return (lax.broadcasted_iota(I32, (PEER_PICKS, LANES), 0)
            == lax.broadcasted_iota(I32, (PEER_PICKS, LANES), 1))


STAGE_TOKENS = 16
TOKENS_PER_GROUP = 2 * STAGE_TOKENS


def _stage_copy(idx_ref, stage, bufs, sems, b):
    rows = pl.ds(pl.multiple_of(stage * STAGE_TOKENS, STAGE_TOKENS), STAGE_TOKENS)
    return pltpu.make_async_copy(idx_ref.at[rows], bufs[b], sems.at[b])


def _for_group_tokens(g, n_stages, idx_ref, bufs, sems, token_fn):
    for b in range(2):
        stage = 2 * g + b
        _stage_copy(idx_ref, stage, bufs, sems, b).wait()
        for tt in range(STAGE_TOKENS):
            local = b * STAGE_TOKENS + tt
            token_fn(local, g * TOKENS_PER_GROUP + local, lambda k, b=b, tt=tt: bufs[b][tt, k])

        @pl.when(stage + 2 < n_stages)
        def _():
            _stage_copy(idx_ref, stage + 2, bufs, sems, b).start()


def _start_first_stages(idx_ref, bufs, sems):
    for b in range(2):
        _stage_copy(idx_ref, b, bufs, sems, b).start()


def _peer_u_kernel(idx_ref, h_ref, gate_ref, tab_ref, act_ref, slot_ref, stage_a, stage_b, sems):
    tm = gate_ref.shape[0]
    slot_rows = PEER_PICKS * ROW_SUBLANES
    diag = _pick_diag()
    bufs = (stage_a, stage_b)
    _start_first_stages(idx_ref, bufs, sems)

    def token(local, t, pick_offset):
        hv = h_ref[pl.ds(pl.multiple_of(t * SUBLANES, SUBLANES), SUBLANES), :]
        h_lo = jnp.concatenate([hv[0:ROW_SUBLANES]] * 2, axis=0)
        h_hi = jnp.concatenate([hv[ROW_SUBLANES:SUBLANES]] * 2, axis=0)
        for k in range(0, PEER_PICKS, 2):
            w = jnp.concatenate([_gather_words(tab_ref, pick_offset(k)),
                                 _gather_words(tab_ref, pick_offset(k + 1))], axis=0)
            lo, hi = _unpack_words(w)
            r0 = local * slot_rows + k * ROW_SUBLANES
            slot_ref[r0:r0 + SUBLANES, :] = lo * h_lo + hi * h_hi

    def group(g, carry):
        _for_group_tokens(g, tm // STAGE_TOKENS, idx_ref, bufs, sems, token)
        rows = []
        for tt in range(TOKENS_PER_GROUP):
            r = slot_ref[pl.ds(tt * slot_rows, PEER_PICKS, stride=ROW_SUBLANES), :]
            for s in range(1, ROW_SUBLANES):
                r = r + slot_ref[pl.ds(tt * slot_rows + s, PEER_PICKS, stride=ROW_SUBLANES), :]
            rows.append(jnp.sum(jnp.where(diag, _lane_sums(r), 0.0), axis=0, keepdims=True))
        off = pl.multiple_of(g * TOKENS_PER_GROUP, TOKENS_PER_GROUP)
        a = jnp.concatenate(rows, axis=0)
        act_ref[pl.ds(off, TOKENS_PER_GROUP), :] = _gelu(a) * gate_ref[pl.ds(off, TOKENS_PER_GROUP), :]
        return carry

    lax.fori_loop(0, tm // TOKENS_PER_GROUP, group, 0)


def _stage_scratch():
    stage = pltpu.SMEM((STAGE_TOKENS, PEER_PICKS), I32)
    return [stage, stage, pltpu.SemaphoreType.DMA((2,))]


def _peer_u(idx, h2_split, gate, tab, tm=512):
    n = idx.shape[0]
    tm = min(tm, n)
    picks = pl.BlockSpec((tm, PEER_PICKS), lambda i: (i, 0))
    return pl.pallas_call(
        _peer_u_kernel,
        grid=(n // tm,),
        in_specs=[picks, _row_split_spec(tm), picks, pl.BlockSpec(memory_space=pltpu.VMEM)],
        out_specs=picks,
        out_shape=jax.ShapeDtypeStruct((n, PEER_PICKS), F32),
        scratch_shapes=[pltpu.VMEM((TOKENS_PER_GROUP * PEER_PICKS * ROW_SUBLANES, LANES), F32), *_stage_scratch()],
        compiler_params=_params("arbitrary"),
        name="peer_u",
    )(idx, h2_split, gate, tab)


N_ACC = 4


def _peer_v_kernel(idx_ref, act_ref, x_ref, g_ref, tab_ref, out_ref, actb_ref, x2_ref, stage_a, stage_b, sems,
                   *, normalize):
    tm = idx_ref.shape[0]
    diag = _pick_diag()
    bufs = (stage_a, stage_b)
    _start_first_stages(idx_ref, bufs, sems)

    def token(local, t, pick_offset):
        acc_lo = [jnp.zeros((ROW_SUBLANES, LANES), F32) for _ in range(N_ACC)]
        acc_hi = [jnp.zeros((ROW_SUBLANES, LANES), F32) for _ in range(N_ACC)]
        for k in range(PEER_PICKS):
            lo, hi = _gather_row(tab_ref, pick_offset(k))
            a = actb_ref[local * PEER_PICKS + k:local * PEER_PICKS + k + 1, :]
            acc_lo[k % N_ACC] = acc_lo[k % N_ACC] + a * lo
            acc_hi[k % N_ACC] = acc_hi[k % N_ACC] + a * hi
        lo = (acc_lo[0] + acc_lo[1]) + (acc_lo[2] + acc_lo[3])
        hi = (acc_hi[0] + acc_hi[1]) + (acc_hi[2] + acc_hi[3])
        r0 = pl.multiple_of(t * SUBLANES, SUBLANES)
        r1 = pl.multiple_of(t * SUBLANES + ROW_SUBLANES, ROW_SUBLANES)
        x2_ref[pl.ds(r0, ROW_SUBLANES), :] = x_ref[pl.ds(r0, ROW_SUBLANES), :] + lo
        x2_ref[pl.ds(r1, ROW_SUBLANES), :] = x_ref[pl.ds(r1, ROW_SUBLANES), :] + hi

    def group(g, carry):
        off = pl.multiple_of(g * TOKENS_PER_GROUP, TOKENS_PER_GROUP)
        for tt in range(TOKENS_PER_GROUP):
            spread = jnp.where(diag, act_ref[pl.ds(off + tt, 1), :], 0.0)
            actb_ref[tt * PEER_PICKS:(tt + 1) * PEER_PICKS, :] = _lane_sums(spread)
        _for_group_tokens(g, tm // STAGE_TOKENS, idx_ref, bufs, sems, token)
        return carry

    lax.fori_loop(0, tm // TOKENS_PER_GROUP, group, 0)
    x2 = _load_row_split(x2_ref)
    out_ref[...] = _rms(x2, g_ref[...]) if normalize else x2


def _peer_v(idx, act, x1_split, tab, final_g, tm=512):
    n = idx.shape[0]
    tm = min(tm, n)
    normalize = final_g is not None
    g = final_g if normalize else jnp.ones((1, D_MODEL), F32)
    picks = pl.BlockSpec((tm, PEER_PICKS), lambda i: (i, 0))
    return pl.pallas_call(
        functools.partial(_peer_v_kernel, normalize=normalize),
        grid=(n // tm,),
        in_specs=[picks, picks, _row_split_spec(tm), _full((1, D_MODEL)), pl.BlockSpec(memory_space=pltpu.VMEM)],
        out_specs=pl.BlockSpec((tm, D_MODEL), lambda i: (i, 0)),
        out_shape=jax.ShapeDtypeStruct((n, D_MODEL), F32),
        scratch_shapes=[pltpu.VMEM((TOKENS_PER_GROUP * PEER_PICKS, LANES), F32),
                        pltpu.VMEM((tm * SUBLANES, LANES), F32), *_stage_scratch()],
        compiler_params=_params("arbitrary"),
        name="peer_v",
    )(idx, act, x1_split, g, tab)


def kernel(x, norm_mix_g, w_in, sgu_norm_g, sgu_w, sgu_b, w_branch_a, w_branch_b, w_out,
           norm_ffn_g, peer_wq, peer_subkeys, peer_u, peer_v, norm_final_g):
    b, s, d = x.shape
    n = b * s
    x2 = x.reshape(n, d)
    for l in range(w_in.shape[0]):
        u, v, ga, gb, *qkv_views = _in_proj(x2, norm_mix_g[l].reshape(1, d), w_in[l].astype(BF16))
        bs_b = jnp.broadcast_to(sgu_b[l][:, :, None], (SGU_GROUPS, SGU_CHUNK, LANES))
        pa = _sgu(u, v, ga, sgu_norm_g[l].reshape(1, SGU_WIDTH), sgu_w[l].astype(BF16), bs_b,
                  w_branch_a[l].astype(BF16))
        os_, ls_ = [], []
        for g, (window, dilation) in enumerate(ATT_PATTERNS):
            qv, kv, vv = qkv_views[3 * g:3 * g + 3]
            o, lse = _attention_pattern(qv, kv, vv, b, dilation, window // (2 * dilation))
            os_.append(o)
            ls_.append(lse)
        x1, h2 = _merge(x2, pa, gb, os_, ls_, w_branch_b[l].astype(BF16), w_out[l].astype(BF16),
                        norm_ffn_g[l].reshape(1, d))
        sk = peer_subkeys[l].reshape(2 * PEER_HEADS, PEER_N_KEYS, PEER_HALF)
        sk_hi = sk.astype(BF16)
        sk_lo = (sk - sk_hi.astype(F32)).astype(BF16)
        idx, gate = _peer_topk(h2, peer_wq[l].astype(BF16), sk_hi, sk_lo)
        act = _peer_u(idx, h2, gate, _pack_table(peer_u[l]))
        last = l == w_in.shape[0] - 1
        x2 = _peer_v(idx, act, x1, _pack_table(peer_v[l]), norm_final_g.reshape(1, d) if last else None)
    return x2.reshape(b, s, d)
```

```python
import functools

import numpy as np
import jax
import jax.numpy as jnp
from jax import lax
from jax.experimental import pallas as pl
from jax.experimental.pallas import tpu as pltpu

F32 = jnp.float32
BF16 = jnp.bfloat16
I32 = jnp.int32

D_MODEL = 1024
EPS = 1e-6
NEG_INF = -1e30
SGU_CHUNK = 128
SGU_GROUPS = 8
SGU_WIDTH = 1024
ATT_HEADS = 8
ATT_HEAD_DIM = 64
ATT_PATTERNS = ((128, 1), (512, 4), (2048, 16))
ATT_GROUPS = 3
ATT_KV_WIDTH = ATT_HEADS * ATT_HEAD_DIM
ATT_Q_WIDTH = ATT_GROUPS * ATT_KV_WIDTH
PEER_HEADS = 8
PEER_N_KEYS = 128
PEER_N_EXPERTS = PEER_N_KEYS * PEER_N_KEYS
PEER_HALF = 128
PEER_TOPK = 16
PEER_PICKS = PEER_HEADS * PEER_TOPK

LANES = 128
SUBLANES = 8
ROW_WORDS = D_MODEL // 2
ROW_SUBLANES = ROW_WORDS // LANES
VMEM_LIMIT = 56 * 1024 * 1024

IN_CHUNK = 512
ATT_DILATIONS = tuple(d for _, d in ATT_PATTERNS)
_N_PLAIN = 4
_IN_LAYOUT = (
    ("gelu", (0, 0), ()), ("gelu", (0, 1), ()), ("gelu", (1, 0), ()), ("gelu", (1, 1), ()),
    *((None, None, ((_N_PLAIN + 3 * g, d),)) for g, d in enumerate(ATT_DILATIONS)),
    (None, None, tuple((_N_PLAIN + 3 * g + 1, d) for g, d in enumerate(ATT_DILATIONS))),
    (None, None, tuple((_N_PLAIN + 3 * g + 2, d) for g, d in enumerate(ATT_DILATIONS))),
    ("sigmoid", (2, 0), ()), ("sigmoid", (2, 1), ()), ("sigmoid", (3, 0), ()), ("sigmoid", (3, 1), ()),
)


def _params(*sem):
    return pltpu.CompilerParams(dimension_semantics=sem, vmem_limit_bytes=VMEM_LIMIT)


def _gelu(y):
    return 0.5 * y * (1.0 + lax.erf(y * np.float32(1.0 / np.sqrt(2.0))))


def _rms(x, g):
    return x * lax.rsqrt(jnp.mean(x * x, axis=-1, keepdims=True) + EPS) * g


def _full(shape):
    return pl.BlockSpec(shape, lambda *_: (0,) * len(shape))


def _store_row_split(ref, x):
    tm = x.shape[0]
    for s in range(SUBLANES):
        ref[pl.ds(s, tm, stride=SUBLANES), :] = x[:, s * LANES:(s + 1) * LANES]


def _load_row_split(ref):
    tm = ref.shape[0] // SUBLANES
    return jnp.concatenate([ref[pl.ds(s, tm, stride=SUBLANES), :] for s in range(SUBLANES)], axis=1)


def _row_split_spec(tm):
    return pl.BlockSpec((tm * SUBLANES, LANES), lambda i: (i, 0))


def _in_proj_kernel(x_ref, g_ref, w_ref, *refs):
    out_refs, ybuf_ref = refs[:-1], refs[-1]
    h = _rms(x_ref[...], g_ref[...]).astype(BF16)
    for c, (epi, plain, views) in enumerate(_IN_LAYOUT):
        y = jnp.dot(h, w_ref[:, c * IN_CHUNK:(c + 1) * IN_CHUNK], preferred_element_type=F32)
        if epi == "gelu":
            y = _gelu(y)
        elif epi == "sigmoid":
            y = jax.nn.sigmoid(y)
        if plain is not None:
            slot, sub = plain
            out_refs[slot][:, sub * IN_CHUNK:(sub + 1) * IN_CHUNK] = y.astype(BF16)
        if any(d > 1 for _, d in views):
            for j in range(IN_CHUNK // LANES):
                ybuf_ref[j] = y[:, j * LANES:(j + 1) * LANES]
        for slot, d in views:
            ref = out_refs[slot]
            if d == 1:
                ref[...] = y.astype(BF16)
                continue
            for r in range(d):
                for j in range(IN_CHUNK // LANES):
                    c0 = r * IN_CHUNK + j * LANES
                    ref[:, c0:c0 + LANES] = ybuf_ref[j, pl.ds(r, ref.shape[0], stride=d), :].astype(BF16)


def _in_proj(x2d, g, w_bf16, tm=512):
    n = x2d.shape[0]
    width = w_bf16.shape[1]
    shapes = [(n, SGU_WIDTH), (n, SGU_WIDTH), (n, D_MODEL), (n, D_MODEL)]
    blocks = [(tm, SGU_WIDTH), (tm, SGU_WIDTH), (tm, D_MODEL), (tm, D_MODEL)]
    for d in ATT_DILATIONS:
        shapes += [(n // d, d * IN_CHUNK)] * 3
        blocks += [(tm // d, d * IN_CHUNK)] * 3
    return pl.pallas_call(
        _in_proj_kernel,
        grid=(n // tm,),
        in_specs=[pl.BlockSpec((tm, D_MODEL), lambda i: (i, 0)),
                  _full((1, D_MODEL)),
                  pl.BlockSpec((D_MODEL, width), lambda i: (0, 0), pipeline_mode=pl.Buffered(1))],
        out_specs=[pl.BlockSpec(blk, lambda i: (i, 0)) for blk in blocks],
        out_shape=[jax.ShapeDtypeStruct(shp, BF16) for shp in shapes],
        scratch_shapes=[pltpu.VMEM((IN_CHUNK // LANES, tm, LANES), F32)],
        compiler_params=_params("parallel"),
        name="in_proj",
    )(x2d, g, w_bf16)


def _sgu_kernel(u_ref, v_ref, ga_ref, ng_ref, ws_ref, bs_ref, wa_ref, out_ref):
    tm = u_ref.shape[0]
    vn = _rms(v_ref[...].astype(F32), ng_ref[...]).astype(BF16)
    chunks = []
    for c in range(tm // SGU_CHUNK):
        cols = []
        for g in range(SGU_GROUPS):
            blk = vn[c * SGU_CHUNK:(c + 1) * SGU_CHUNK, g * LANES:(g + 1) * LANES]
            cols.append(jnp.dot(ws_ref[g], blk, preferred_element_type=F32) + bs_ref[g])
        chunks.append(jnp.concatenate(cols, axis=1))
    mixed = jnp.concatenate(chunks, axis=0)
    ya = (u_ref[...].astype(F32) * mixed).astype(BF16)
    pa = jnp.dot(ya, wa_ref[...], preferred_element_type=F32) * ga_ref[...].astype(F32)
    out_ref[...] = pa.astype(BF16)


def _sgu(u, v, ga, ng, ws_bf16, bs_b, wa_bf16, tm=512):
    n = u.shape[0]
    row = pl.BlockSpec((tm, SGU_WIDTH), lambda i: (i, 0))
    return pl.pallas_call(
        _sgu_kernel,
        grid=(n // tm,),
        in_specs=[row, row, row, _full((1, SGU_WIDTH)),
                  _full((SGU_GROUPS, SGU_CHUNK, SGU_CHUNK)),
                  _full((SGU_GROUPS, SGU_CHUNK, LANES)),
                  _full((SGU_WIDTH, D_MODEL))],
        out_specs=pl.BlockSpec((tm, D_MODEL), lambda i: (i, 0)),
        out_shape=jax.ShapeDtypeStruct((n, D_MODEL), BF16),
        compiler_params=_params("parallel"),
        name="sgu",
    )(u, v, ga, ng, ws_bf16, bs_b, wa_bf16)


def _attn_kernel(q_ref, kp_ref, kc_ref, kn_ref, vp_ref, vc_ref, vn_ref, o_ref, lse_ref,
                 *, dilation, n_side, sub_len):
    tq = q_ref.shape[0]
    win = tq + 2 * n_side
    i = pl.program_id(2)
    k = jnp.concatenate([kp_ref[...], kc_ref[...], kn_ref[...]], axis=0)
    v = jnp.concatenate([vp_ref[...], vc_ref[...], vn_ref[...]], axis=0)
    row = lax.broadcasted_iota(I32, (tq, win), 0)
    col = lax.broadcasted_iota(I32, (tq, win), 1)
    rel = col - n_side - row
    pos = i * tq - n_side + col
    valid = (jnp.abs(rel) <= n_side) & (pos >= 0) & (pos < sub_len)
    negdist = -(jnp.abs(rel) * dilation).astype(F32)
    lane = lax.broadcasted_iota(I32, (tq, LANES), 1)
    low_half = lane < ATT_HEAD_DIM
    scale = np.float32(ATT_HEAD_DIM ** -0.5)
    for pair in range(ATT_HEADS // 2):
        sl = slice(pair * LANES, (pair + 1) * LANES)
        qp, kpair, vpair = q_ref[:, sl], k[:, sl], v[:, sl]
        outs, lses = [], []
        for j in range(2):
            slope = np.float32(2.0 ** (-8.0 * (pair * 2 + j + 1) / ATT_HEADS))
            qm = jnp.where(low_half if j == 0 else jnp.logical_not(low_half), qp, jnp.zeros_like(qp))
            s = lax.dot_general(qm, kpair, (((1,), (1,)), ((), ())), preferred_element_type=F32)
            s = jnp.where(valid, s * scale + slope * negdist, NEG_INF)
            m = jnp.max(s, axis=-1, keepdims=True)
            p = jnp.exp(s - m)
            l = jnp.sum(p, axis=-1, keepdims=True)
            o = jnp.dot(p.astype(BF16), vpair, preferred_element_type=F32)
            outs.append(o / l)
            lses.append(jnp.broadcast_to(m + jnp.log(l), (tq, LANES)))
        o_ref[:, sl] = jnp.where(low_half, outs[0], outs[1]).astype(o_ref.dtype)
        lse_ref[:, sl] = jnp.where(low_half, lses[0], lses[1])


ATT_QUERY_TILE = 128


def _attention_pattern(qv, kv, vv, batch, dilation, n_side):
    sub_len = qv.shape[0] // batch
    tq = min(ATT_QUERY_TILE, sub_len)
    kvw = ATT_KV_WIDTH
    b = batch
    qv, kv, vv = (t.reshape(b, sub_len, dilation * kvw) for t in (qv, kv, vv))
    per = tq // n_side
    last = sub_len // n_side - 1
    cur = pl.BlockSpec((None, tq, kvw), lambda bi, r, i: (bi, i, r))
    prev = pl.BlockSpec((None, n_side, kvw), lambda bi, r, i: (bi, jnp.maximum(i * per - 1, 0), r))
    nxt = pl.BlockSpec((None, n_side, kvw), lambda bi, r, i: (bi, jnp.minimum((i + 1) * per, last), r))
    o, lse = pl.pallas_call(
        functools.partial(_attn_kernel, dilation=dilation, n_side=n_side, sub_len=sub_len),
        grid=(b, dilation, sub_len // tq),
        in_specs=[cur, prev, cur, nxt, prev, cur, nxt],
        out_specs=[cur, cur],
        out_shape=[jax.ShapeDtypeStruct((b, sub_len, dilation * kvw), BF16),
                   jax.ShapeDtypeStruct((b, sub_len, dilation * kvw), F32)],
        compiler_params=_params("parallel", "parallel", "parallel"),
        name=f"attn_d{dilation}",
    )(qv, kv, kv, kv, vv, vv, vv)
    return o.reshape(b * sub_len, dilation * kvw), lse.reshape(b * sub_len, dilation * kvw)


def _from_residue_view(blk_ref, tmp_ref, d):
    if d == 1:
        return blk_ref[...].astype(F32)
    n_tiles = IN_CHUNK // LANES
    for r in range(d):
        for j in range(n_tiles):
            c0 = r * IN_CHUNK + j * LANES
            tmp_ref[j, pl.ds(r, blk_ref.shape[0], stride=d), :] = blk_ref[:, c0:c0 + LANES].astype(F32)
    return jnp.concatenate([tmp_ref[j] for j in range(n_tiles)], axis=1)


def _merge_kernel(x_ref, pa_ref, gb_ref, o1_ref, o2_ref, o3_ref, l1_ref, l2_ref, l3_ref,
                  wb_ref, wo_ref, g_ref, x1_ref, h2_ref, *tmp_refs):
    tmp = iter(tmp_refs)
    os_, ls_ = [], []
    for o_ref, l_ref, d in zip((o1_ref, o2_ref, o3_ref), (l1_ref, l2_ref, l3_ref), ATT_DILATIONS):
        os_.append(_from_residue_view(o_ref, next(tmp) if d > 1 else None, d))
        ls_.append(_from_residue_view(l_ref, next(tmp) if d > 1 else None, d))
    l1, l2, l3 = ls_
    m = jnp.maximum(jnp.maximum(l1, l2), l3)
    w1, w2, w3 = jnp.exp(l1 - m), jnp.exp(l2 - m), jnp.exp(l3 - m)
    yb = (w1 * os_[0] + w2 * os_[1] + w3 * os_[2]) / (w1 + w2 + w3)
    pb = jnp.dot(yb.astype(BF16), wb_ref[...], preferred_element_type=F32) * gb_ref[...].astype(F32)
    merged = (pa_ref[...].astype(F32) + pb).astype(BF16)
    x1 = x_ref[...] + jnp.dot(merged, wo_ref[...], preferred_element_type=F32)
    _store_row_split(x1_ref, x1)
    _store_row_split(h2_ref, _rms(x1, g_ref[...]))


def _merge(x2d, pa, gb, os_, ls_, wb_bf16, wo_bf16, g, tm=512):
    n = x2d.shape[0]
    wide = pl.BlockSpec((tm, D_MODEL), lambda i: (i, 0))
    views = [pl.BlockSpec((tm // d, d * ATT_KV_WIDTH), lambda i: (i, 0)) for d in ATT_DILATIONS]
    n_tmp = 2 * sum(d > 1 for d in ATT_DILATIONS)
    return pl.pallas_call(
        _merge_kernel,
        grid=(n // tm,),
        in_specs=[wide, wide, wide, *views, *views,
                  _full((ATT_KV_WIDTH, D_MODEL)), _full((D_MODEL, D_MODEL)), _full((1, D_MODEL))],
        out_specs=[_row_split_spec(tm), _row_split_spec(tm)],
        out_shape=[jax.ShapeDtypeStruct((n * SUBLANES, LANES), F32)] * 2,
        scratch_shapes=[pltpu.VMEM((ATT_KV_WIDTH // LANES, tm, LANES), F32)] * n_tmp,
        compiler_params=_params("parallel"),
        name="merge",
    )(x2d, pa, gb, *os_, *ls_, wb_bf16, wo_bf16, g)


def _topk_rows(s, k, payload=None):
    rows = lax.broadcasted_iota(I32, s.shape, 0).astype(F32)
    vals, ids = [], []
    for _ in range(k):
        m = jnp.max(s, axis=0, keepdims=True)
        am = jnp.min(jnp.where(s == m, rows, np.float32(s.shape[0])), axis=0, keepdims=True)
        hit = rows == am
        vals.append(m)
        if payload is None:
            ids.append(am)
        else:
            ids.append(jnp.max(jnp.where(hit, payload, -1.0), axis=0, keepdims=True))
        s = jnp.where(hit, -jnp.inf, s)
    return jnp.concatenate(vals, axis=0), jnp.concatenate(ids, axis=0)


def _pair_candidates(s0, s1, i0, i1):
    vals, ids = [], []
    for a in range(SUBLANES):
        n_b = PEER_TOPK // (a + 1)
        rows = PEER_TOPK if n_b > SUBLANES else SUBLANES
        v = s0[a:a + 1] + s1[0:rows]
        if n_b < rows:
            v = jnp.where(lax.broadcasted_iota(I32, v.shape, 0) < n_b, v, -jnp.inf)
        vals.append(v)
        ids.append(i0[a:a + 1] * PEER_N_KEYS + i1[0:rows])
    vals.append(s0[SUBLANES:PEER_TOPK] + s1[0:1])
    ids.append(i0[SUBLANES:PEER_TOPK] * PEER_N_KEYS + i1[0:1])
    return jnp.concatenate(vals, axis=0), jnp.concatenate(ids, axis=0)


def _split_bf16(a):
    hi = a.astype(BF16)
    return hi, (a - hi.astype(F32)).astype(BF16)


def _peer_topk_kernel(h_ref, wq_ref, skh_ref, skl_ref, idx_ref, gate_ref):
    q = jnp.dot(_load_row_split(h_ref).astype(BF16), wq_ref[...], preferred_element_type=F32)
    nt = (((1,), (1,)), ((), ()))
    idx_rows, gate_rows = [], []
    for hd in range(PEER_HEADS):
        top_s, top_i = [], []
        for p in range(2):
            hp = hd * 2 + p
            q_hi, q_lo = _split_bf16(q[:, hp * PEER_HALF:(hp + 1) * PEER_HALF])
            kh, kl = skh_ref[hp], skl_ref[hp]
            st = (lax.dot_general(kh, q_hi, nt, preferred_element_type=F32)
                  + lax.dot_general(kh, q_lo, nt, preferred_element_type=F32)
                  + lax.dot_general(kl, q_hi, nt, preferred_element_type=F32))
            ts, ti = _topk_rows(st, PEER_TOPK)
            top_s.append(ts)
            top_i.append(ti)
        cand, cidx = _pair_candidates(top_s[0], top_s[1], top_i[0], top_i[1])
        best_s, expert = _topk_rows(cand, PEER_TOPK, payload=cidx)
        e = jnp.exp(best_s - best_s[0:1])
        gate_rows.append(e / jnp.sum(e, axis=0, keepdims=True))
        idx_rows.append(expert * np.float32(ROW_SUBLANES))
    idx_ref[...] = jnp.concatenate(idx_rows, axis=0).T.astype(I32)
    gate_ref[...] = jnp.concatenate(gate_rows, axis=0).T


def _peer_topk(h2_split, wq_bf16, sk_hi, sk_lo, tm=256):
    n = h2_split.shape[0] // SUBLANES
    qw = wq_bf16.shape[1]
    out = pl.BlockSpec((tm, PEER_PICKS), lambda i: (i, 0))
    return pl.pallas_call(
        _peer_topk_kernel,
        grid=(n // tm,),
        in_specs=[_row_split_spec(tm),
                  _full((D_MODEL, qw)),
                  _full((2 * PEER_HEADS, PEER_N_KEYS, PEER_HALF)),
                  _full((2 * PEER_HEADS, PEER_N_KEYS, PEER_HALF))],
        out_specs=[out, out],
        out_shape=[jax.ShapeDtypeStruct((n, PEER_PICKS), I32), jax.ShapeDtypeStruct((n, PEER_PICKS), F32)],
        compiler_params=_params("parallel"),
        name="peer_topk",
    )(h2_split, wq_bf16, sk_hi, sk_lo)


def _pack_table(tab):
    lo16 = lax.bitcast_convert_type(tab[:, :ROW_WORDS].astype(BF16), jnp.uint16).astype(jnp.uint32)
    full = lax.bitcast_convert_type(tab[:, ROW_WORDS:], jnp.uint32)
    sign = full & jnp.uint32(0x80000000)
    mag = (full & jnp.uint32(0x7FFFFFFF)) + jnp.uint32(0x8000)
    hi16 = jnp.where(mag >= lo16, (mag - lo16) >> 16, jnp.uint32(0))
    word = sign | (hi16 << 16) | lo16
    return lax.bitcast_convert_type(word, I32).reshape(tab.shape[0] * ROW_SUBLANES, LANES)


def _gather_words(tab_ref, e4):
    return tab_ref[pl.ds(pl.multiple_of(e4, ROW_SUBLANES), ROW_SUBLANES), :]


def _unpack_words(w):
    return lax.bitcast_convert_type(w << 16, F32), lax.bitcast_convert_type(w, F32)


def _gather_row(tab_ref, e4):
    return _unpack_words(_gather_words(tab_ref, e4))


def _lane_sums(rows):
    return jnp.dot(rows.astype(BF16), jnp.ones((LANES, LANES), BF16), preferred_element_type=F32)


def _pick_diag():
    return (lax.broadcasted_iota(I32, (PEER_PICKS, LANES), 0)
            == lax.broadcasted_iota(I32, (PEER_PICKS, LANES), 1))


STAGE_TOKENS = 16
TOKENS_PER_GROUP = 2 * STAGE_TOKENS


def _stage_copy(idx_ref, stage, bufs, sems, b):
    rows = pl.ds(pl.multiple_of(stage * STAGE_TOKENS, STAGE_TOKENS), STAGE_TOKENS)
    return pltpu.make_async_copy(idx_ref.at[rows], bufs[b], sems.at[b])


def _for_group_tokens(g, n_stages, idx_ref, bufs, sems, token_fn):
    for b in range(2):
        stage = 2 * g + b
        _stage_copy(idx_ref, stage, bufs, sems, b).wait()
        for tt in range(STAGE_TOKENS):
            local = b * STAGE_TOKENS + tt
            token_fn(local, g * TOKENS_PER_GROUP + local, lambda k, b=b, tt=tt: bufs[b][tt, k])
        _stage_copy(idx_ref, jnp.minimum(stage + 2, n_stages - 1), bufs, sems, b).start()


def _finish_stages(idx_ref, bufs, sems):
    for b in range(2):
        _stage_copy(idx_ref, 0, bufs, sems, b).wait()


def _start_first_stages(idx_ref, bufs, sems):
    for b in range(2):
        _stage_copy(idx_ref, b, bufs, sems, b).start()


def _peer_u_kernel(idx_ref, h_ref, gate_ref, tab_ref, act_ref, slot_ref, stage_a, stage_b, sems):
    tm = gate_ref.shape[0]
    slot_rows = PEER_PICKS * ROW_SUBLANES
    diag = _pick_diag()
    bufs = (stage_a, stage_b)
    _start_first_stages(idx_ref, bufs, sems)

    def token(local, t, pick_offset):
        hv = h_ref[pl.ds(pl.multiple_of(t * SUBLANES, SUBLANES), SUBLANES), :]
        h_lo = jnp.concatenate([hv[0:ROW_SUBLANES]] * 2, axis=0)
        h_hi = jnp.concatenate([hv[ROW_SUBLANES:SUBLANES]] * 2, axis=0)
        for k in range(0, PEER_PICKS, 2):
            w = jnp.concatenate([_gather_words(tab_ref, pick_offset(k)),
                                 _gather_words(tab_ref, pick_offset(k + 1))], axis=0)
            lo, hi = _unpack_words(w)
            r0 = local * slot_rows + k * ROW_SUBLANES
            slot_ref[r0:r0 + SUBLANES, :] = lo * h_lo + hi * h_hi

    def group(g, carry):
        _for_group_tokens(g, tm // STAGE_TOKENS, idx_ref, bufs, sems, token)
        rows = []
        for tt in range(TOKENS_PER_GROUP):
            r = slot_ref[pl.ds(tt * slot_rows, PEER_PICKS, stride=ROW_SUBLANES), :]
            for s in range(1, ROW_SUBLANES):
                r = r + slot_ref[pl.ds(tt * slot_rows + s, PEER_PICKS, stride=ROW_SUBLANES), :]
            rows.append(jnp.sum(jnp.where(diag, _lane_sums(r), 0.0), axis=0, keepdims=True))
        off = pl.multiple_of(g * TOKENS_PER_GROUP, TOKENS_PER_GROUP)
        a = jnp.concatenate(rows, axis=0)
        act_ref[pl.ds(off, TOKENS_PER_GROUP), :] = _gelu(a) * gate_ref[pl.ds(off, TOKENS_PER_GROUP), :]
        return carry

    lax.fori_loop(0, tm // TOKENS_PER_GROUP, group, 0)
    _finish_stages(idx_ref, bufs, sems)


def _stage_scratch():
    stage = pltpu.SMEM((STAGE_TOKENS, PEER_PICKS), I32)
    return [stage, stage, pltpu.SemaphoreType.DMA((2,))]


def _peer_u(idx, h2_split, gate, tab, tm=512):
    n = idx.shape[0]
    tm = min(tm, n)
    picks = pl.BlockSpec((tm, PEER_PICKS), lambda i: (i, 0))
    return pl.pallas_call(
        _peer_u_kernel,
        grid=(n // tm,),
        in_specs=[picks, _row_split_spec(tm), picks, pl.BlockSpec(memory_space=pltpu.VMEM)],
        out_specs=picks,
        out_shape=jax.ShapeDtypeStruct((n, PEER_PICKS), F32),
        scratch_shapes=[pltpu.VMEM((TOKENS_PER_GROUP * PEER_PICKS * ROW_SUBLANES, LANES), F32), *_stage_scratch()],
        compiler_params=_params("arbitrary"),
        name="peer_u",
    )(idx, h2_split, gate, tab)


PICK_ROWS = 2 * ROW_SUBLANES


def _pack_table_rows(tab):
    e = tab.shape[0]
    bits = lax.bitcast_convert_type(tab.astype(BF16), jnp.uint16).astype(jnp.uint32)
    bits = bits.reshape(e, ROW_SUBLANES, 2, LANES)
    word = bits[:, :, 0, :] | (bits[:, :, 1, :] << 16)
    return lax.bitcast_convert_type(word, I32).reshape(e * ROW_SUBLANES, LANES)


def _pick_spread_matrix():
    k = lax.broadcasted_iota(I32, (PEER_PICKS, PEER_PICKS * PICK_ROWS), 0)
    c = lax.broadcasted_iota(I32, (PEER_PICKS, PEER_PICKS * PICK_ROWS), 1)
    return (c // PICK_ROWS == k).astype(BF16)


def _peer_v_kernel(idx_ref, act_ref, x_ref, g_ref, tab_ref, spread_ref, out_ref, slot_ref, arep_ref, x2_ref,
                   stage_a, stage_b, sems, *, normalize):
    tm = idx_ref.shape[0]
    slot_rows = PEER_PICKS * ROW_SUBLANES
    cols = PEER_PICKS * PICK_ROWS
    keep = (lax.broadcasted_iota(I32, (PICK_ROWS, cols), 1) % PICK_ROWS
            == lax.broadcasted_iota(I32, (PICK_ROWS, cols), 0))
    bufs = (stage_a, stage_b)
    _start_first_stages(idx_ref, bufs, sems)

    def token(local, t, pick_offset):
        base = (local % STAGE_TOKENS) * slot_rows
        for k in range(PEER_PICKS):
            r0 = base + k * ROW_SUBLANES
            slot_ref[r0:r0 + ROW_SUBLANES, :] = _gather_words(tab_ref, pick_offset(k))
        rows = pltpu.bitcast(slot_ref[base:base + slot_rows, :], BF16)
        w_hi = jnp.where(keep, arep_ref[pl.ds(t, 1), :], 0.0)
        w_lo = jnp.where(keep, arep_ref[pl.ds(tm + t, 1), :], 0.0)
        o = jnp.dot(jnp.concatenate([w_hi, w_lo], axis=0).astype(BF16), rows, preferred_element_type=F32)
        r = pl.multiple_of(t * SUBLANES, SUBLANES)
        x2_ref[pl.ds(r, SUBLANES), :] = x_ref[pl.ds(r, SUBLANES), :] + (o[0:PICK_ROWS] + o[PICK_ROWS:])

    a_hi, a_lo = _split_bf16(act_ref[...])
    arep_ref[0:tm, :] = jnp.dot(a_hi, spread_ref[...], preferred_element_type=F32)
    arep_ref[tm:2 * tm, :] = jnp.dot(a_lo, spread_ref[...], preferred_element_type=F32)

    def group(g, carry):
        _for_group_tokens(g, tm // STAGE_TOKENS, idx_ref, bufs, sems, token)
        return carry

    lax.fori_loop(0, tm // TOKENS_PER_GROUP, group, 0)
    _finish_stages(idx_ref, bufs, sems)
    x2 = _load_row_split(x2_ref)
    out_ref[...] = _rms(x2, g_ref[...]) if normalize else x2


def _peer_v(idx, act, x1_split, tab, final_g, tm=256):
    n = idx.shape[0]
    tm = min(tm, n)
    normalize = final_g is not None
    g = final_g if normalize else jnp.ones((1, D_MODEL), F32)
    picks = pl.BlockSpec((tm, PEER_PICKS), lambda i: (i, 0))
    cols = PEER_PICKS * PICK_ROWS
    return pl.pallas_call(
        functools.partial(_peer_v_kernel, normalize=normalize),
        grid=(n // tm,),
        in_specs=[picks, picks, _row_split_spec(tm), _full((1, D_MODEL)), pl.BlockSpec(memory_space=pltpu.VMEM),
                  _full((PEER_PICKS, cols))],
        out_specs=pl.BlockSpec((tm, D_MODEL), lambda i: (i, 0)),
        out_shape=jax.ShapeDtypeStruct((n, D_MODEL), F32),
        scratch_shapes=[pltpu.VMEM((STAGE_TOKENS * PEER_PICKS * ROW_SUBLANES, LANES), I32),
                        pltpu.VMEM((2 * tm, cols), F32),
                        pltpu.VMEM((tm * SUBLANES, LANES), F32), *_stage_scratch()],
        compiler_params=_params("arbitrary"),
        name="peer_v",
    )(idx, act, x1_split, g, tab, _pick_spread_matrix())


def kernel(x, norm_mix_g, w_in, sgu_norm_g, sgu_w, sgu_b, w_branch_a, w_branch_b, w_out,
           norm_ffn_g, peer_wq, peer_subkeys, peer_u, peer_v, norm_final_g):
    b, s, d = x.shape
    n = b * s
    x2 = x.reshape(n, d)
    for l in range(w_in.shape[0]):
        u, v, ga, gb, *qkv_views = _in_proj(x2, norm_mix_g[l].reshape(1, d), w_in[l].astype(BF16))
        bs_b = jnp.broadcast_to(sgu_b[l][:, :, None], (SGU_GROUPS, SGU_CHUNK, LANES))
        pa = _sgu(u, v, ga, sgu_norm_g[l].reshape(1, SGU_WIDTH), sgu_w[l].astype(BF16), bs_b,
                  w_branch_a[l].astype(BF16))
        os_, ls_ = [], []
        for g, (window, dilation) in enumerate(ATT_PATTERNS):
            qv, kv, vv = qkv_views[3 * g:3 * g + 3]
            o, lse = _attention_pattern(qv, kv, vv, b, dilation, window // (2 * dilation))
            os_.append(o)
            ls_.append(lse)
        x1, h2 = _merge(x2, pa, gb, os_, ls_, w_branch_b[l].astype(BF16), w_out[l].astype(BF16),
                        norm_ffn_g[l].reshape(1, d))
        sk = peer_subkeys[l].reshape(2 * PEER_HEADS, PEER_N_KEYS, PEER_HALF)
        sk_hi = sk.astype(BF16)
        sk_lo = (sk - sk_hi.astype(F32)).astype(BF16)
        idx, gate = _peer_topk(h2, peer_wq[l].astype(BF16), sk_hi, sk_lo)
        act = _peer_u(idx, h2, gate, _pack_table(peer_u[l]))
        last = l == w_in.shape[0] - 1
        x2 = _peer_v(idx, act, x1, _pack_table_rows(peer_v[l]), norm_final_g.reshape(1, d) if last else None)
    return x2.reshape(b, s, d)
```

```python
import functools

import numpy as np
import jax
import jax.numpy as jnp
from jax import lax
from jax.experimental import pallas as pl
from jax.experimental.pallas import tpu as pltpu

F32 = jnp.float32
BF16 = jnp.bfloat16
I32 = jnp.int32

D_MODEL = 1024
EPS = 1e-6
NEG_INF = -1e30
SGU_CHUNK = 128
SGU_GROUPS = 8
SGU_WIDTH = 1024
ATT_HEADS = 8
ATT_HEAD_DIM = 64
ATT_PATTERNS = ((128, 1), (512, 4), (2048, 16))
ATT_GROUPS = 3
ATT_KV_WIDTH = ATT_HEADS * ATT_HEAD_DIM
ATT_Q_WIDTH = ATT_GROUPS * ATT_KV_WIDTH
PEER_HEADS = 8
PEER_N_KEYS = 128
PEER_N_EXPERTS = PEER_N_KEYS * PEER_N_KEYS
PEER_HALF = 128
PEER_TOPK = 16
PEER_PICKS = PEER_HEADS * PEER_TOPK

LANES = 128
SUBLANES = 8
ROW_SUBLANES = SUBLANES // 2
VMEM_LIMIT = 56 * 1024 * 1024

IN_CHUNK = 512
ATT_DILATIONS = tuple(d for _, d in ATT_PATTERNS)
_N_PLAIN = 4
_IN_LAYOUT = (
    ("gelu", (0, 0), ()), ("gelu", (0, 1), ()), ("gelu", (1, 0), ()), ("gelu", (1, 1), ()),
    *((None, None, ((_N_PLAIN + 3 * g, d),)) for g, d in enumerate(ATT_DILATIONS)),
    (None, None, tuple((_N_PLAIN + 3 * g + 1, d) for g, d in enumerate(ATT_DILATIONS))),
    (None, None, tuple((_N_PLAIN + 3 * g + 2, d) for g, d in enumerate(ATT_DILATIONS))),
    ("sigmoid", (2, 0), ()), ("sigmoid", (2, 1), ()), ("sigmoid", (3, 0), ()), ("sigmoid", (3, 1), ()),
)


def _params(*sem):
    return pltpu.CompilerParams(dimension_semantics=sem, vmem_limit_bytes=VMEM_LIMIT)


def _gelu(y):
    return 0.5 * y * (1.0 + lax.erf(y * np.float32(1.0 / np.sqrt(2.0))))


def _rms(x, g):
    return x * lax.rsqrt(jnp.mean(x * x, axis=-1, keepdims=True) + EPS) * g


def _full(shape):
    return pl.BlockSpec(shape, lambda *_: (0,) * len(shape))


def _store_row_split(ref, x):
    tm = x.shape[0]
    for s in range(SUBLANES):
        ref[pl.ds(s, tm, stride=SUBLANES), :] = x[:, s * LANES:(s + 1) * LANES]


def _load_row_split(ref):
    tm = ref.shape[0] // SUBLANES
    return jnp.concatenate([ref[pl.ds(s, tm, stride=SUBLANES), :] for s in range(SUBLANES)], axis=1)


def _row_split_spec(tm):
    return pl.BlockSpec((tm * SUBLANES, LANES), lambda i: (i, 0))


def _in_proj_kernel(x_ref, g_ref, w_ref, *refs):
    out_refs, ybuf_ref = refs[:-1], refs[-1]
    h = _rms(x_ref[...], g_ref[...]).astype(BF16)
    for c, (epi, plain, views) in enumerate(_IN_LAYOUT):
        y = jnp.dot(h, w_ref[:, c * IN_CHUNK:(c + 1) * IN_CHUNK], preferred_element_type=F32)
        if epi == "gelu":
            y = _gelu(y)
        elif epi == "sigmoid":
            y = jax.nn.sigmoid(y)
        if plain is not None:
            slot, sub = plain
            out_refs[slot][:, sub * IN_CHUNK:(sub + 1) * IN_CHUNK] = y.astype(BF16)
        if any(d > 1 for _, d in views):
            for j in range(IN_CHUNK // LANES):
                ybuf_ref[j] = y[:, j * LANES:(j + 1) * LANES]
        for slot, d in views:
            ref = out_refs[slot]
            if d == 1:
                ref[...] = y.astype(BF16)
                continue
            for r in range(d):
                for j in range(IN_CHUNK // LANES):
                    c0 = r * IN_CHUNK + j * LANES
                    ref[:, c0:c0 + LANES] = ybuf_ref[j, pl.ds(r, ref.shape[0], stride=d), :].astype(BF16)


def _in_proj(x2d, g, w_bf16, tm=512):
    n = x2d.shape[0]
    width = w_bf16.shape[1]
    shapes = [(n, SGU_WIDTH), (n, SGU_WIDTH), (n, D_MODEL), (n, D_MODEL)]
    blocks = [(tm, SGU_WIDTH), (tm, SGU_WIDTH), (tm, D_MODEL), (tm, D_MODEL)]
    for d in ATT_DILATIONS:
        shapes += [(n // d, d * IN_CHUNK)] * 3
        blocks += [(tm // d, d * IN_CHUNK)] * 3
    return pl.pallas_call(
        _in_proj_kernel,
        grid=(n // tm,),
        in_specs=[pl.BlockSpec((tm, D_MODEL), lambda i: (i, 0)),
                  _full((1, D_MODEL)),
                  pl.BlockSpec((D_MODEL, width), lambda i: (0, 0), pipeline_mode=pl.Buffered(1))],
        out_specs=[pl.BlockSpec(blk, lambda i: (i, 0)) for blk in blocks],
        out_shape=[jax.ShapeDtypeStruct(shp, BF16) for shp in shapes],
        scratch_shapes=[pltpu.VMEM((IN_CHUNK // LANES, tm, LANES), F32)],
        compiler_params=_params("parallel"),
        name="in_proj",
    )(x2d, g, w_bf16)


def _sgu_kernel(u_ref, v_ref, ga_ref, ng_ref, ws_ref, bs_ref, wa_ref, out_ref):
    tm = u_ref.shape[0]
    vn = _rms(v_ref[...].astype(F32), ng_ref[...]).astype(BF16)
    chunks = []
    for c in range(tm // SGU_CHUNK):
        cols = []
        for g in range(SGU_GROUPS):
            blk = vn[c * SGU_CHUNK:(c + 1) * SGU_CHUNK, g * LANES:(g + 1) * LANES]
            cols.append(jnp.dot(ws_ref[g], blk, preferred_element_type=F32) + bs_ref[g])
        chunks.append(jnp.concatenate(cols, axis=1))
    mixed = jnp.concatenate(chunks, axis=0)
    ya = (u_ref[...].astype(F32) * mixed).astype(BF16)
    pa = jnp.dot(ya, wa_ref[...], preferred_element_type=F32) * ga_ref[...].astype(F32)
    out_ref[...] = pa.astype(BF16)


def _sgu(u, v, ga, ng, ws_bf16, bs_b, wa_bf16, tm=512):
    n = u.shape[0]
    row = pl.BlockSpec((tm, SGU_WIDTH), lambda i: (i, 0))
    return pl.pallas_call(
        _sgu_kernel,
        grid=(n // tm,),
        in_specs=[row, row, row, _full((1, SGU_WIDTH)),
                  _full((SGU_GROUPS, SGU_CHUNK, SGU_CHUNK)),
                  _full((SGU_GROUPS, SGU_CHUNK, LANES)),
                  _full((SGU_WIDTH, D_MODEL))],
        out_specs=pl.BlockSpec((tm, D_MODEL), lambda i: (i, 0)),
        out_shape=jax.ShapeDtypeStruct((n, D_MODEL), BF16),
        compiler_params=_params("parallel"),
        name="sgu",
    )(u, v, ga, ng, ws_bf16, bs_b, wa_bf16)


def _attn_kernel(q_ref, kp_ref, kc_ref, kn_ref, vp_ref, vc_ref, vn_ref, o_ref, lse_ref,
                 *, dilation, n_side, sub_len):
    tq = q_ref.shape[0]
    win = tq + 2 * n_side
    i = pl.program_id(2)
    k = jnp.concatenate([kp_ref[...], kc_ref[...], kn_ref[...]], axis=0)
    v = jnp.concatenate([vp_ref[...], vc_ref[...], vn_ref[...]], axis=0)
    row = lax.broadcasted_iota(I32, (tq, win), 0)
    col = lax.broadcasted_iota(I32, (tq, win), 1)
    rel = col - n_side - row
    pos = i * tq - n_side + col
    valid = (jnp.abs(rel) <= n_side) & (pos >= 0) & (pos < sub_len)
    negdist = -(jnp.abs(rel) * dilation).astype(F32)
    lane = lax.broadcasted_iota(I32, (tq, LANES), 1)
    low_half = lane < ATT_HEAD_DIM
    scale = np.float32(ATT_HEAD_DIM ** -0.5)
    for pair in range(ATT_HEADS // 2):
        sl = slice(pair * LANES, (pair + 1) * LANES)
        qp, kpair, vpair = q_ref[:, sl], k[:, sl], v[:, sl]
        outs, lses = [], []
        for j in range(2):
            slope = np.float32(2.0 ** (-8.0 * (pair * 2 + j + 1) / ATT_HEADS))
            qm = jnp.where(low_half if j == 0 else jnp.logical_not(low_half), qp, jnp.zeros_like(qp))
            s = lax.dot_general(qm, kpair, (((1,), (1,)), ((), ())), preferred_element_type=F32)
            s = jnp.where(valid, s * scale + slope * negdist, NEG_INF)
            m = jnp.max(s, axis=-1, keepdims=True)
            p = jnp.exp(s - m)
            l = jnp.sum(p, axis=-1, keepdims=True)
            o = jnp.dot(p.astype(BF16), vpair, preferred_element_type=F32)
            outs.append(o / l)
            lses.append(jnp.broadcast_to(m + jnp.log(l), (tq, LANES)))
        o_ref[:, sl] = jnp.where(low_half, outs[0], outs[1]).astype(o_ref.dtype)
        lse_ref[:, sl] = jnp.where(low_half, lses[0], lses[1])


ATT_QUERY_TILE = 128


def _attention_pattern(qv, kv, vv, batch, dilation, n_side):
    sub_len = qv.shape[0] // batch
    tq = min(ATT_QUERY_TILE, sub_len)
    kvw = ATT_KV_WIDTH
    b = batch
    qv, kv, vv = (t.reshape(b, sub_len, dilation * kvw) for t in (qv, kv, vv))
    per = tq // n_side
    last = sub_len // n_side - 1
    cur = pl.BlockSpec((None, tq, kvw), lambda bi, r, i: (bi, i, r))
    prev = pl.BlockSpec((None, n_side, kvw), lambda bi, r, i: (bi, jnp.maximum(i * per - 1, 0), r))
    nxt = pl.BlockSpec((None, n_side, kvw), lambda bi, r, i: (bi, jnp.minimum((i + 1) * per, last), r))
    o, lse = pl.pallas_call(
        functools.partial(_attn_kernel, dilation=dilation, n_side=n_side, sub_len=sub_len),
        grid=(b, dilation, sub_len // tq),
        in_specs=[cur, prev, cur, nxt, prev, cur, nxt],
        out_specs=[cur, cur],
        out_shape=[jax.ShapeDtypeStruct((b, sub_len, dilation * kvw), BF16),
                   jax.ShapeDtypeStruct((b, sub_len, dilation * kvw), F32)],
        compiler_params=_params("parallel", "parallel", "parallel"),
        name=f"attn_d{dilation}",
    )(qv, kv, kv, kv, vv, vv, vv)
    return o.reshape(b * sub_len, dilation * kvw), lse.reshape(b * sub_len, dilation * kvw)


def _from_residue_view(blk_ref, tmp_ref, d):
    if d == 1:
        return blk_ref[...].astype(F32)
    n_tiles = IN_CHUNK // LANES
    for r in range(d):
        for j in range(n_tiles):
            c0 = r * IN_CHUNK + j * LANES
            tmp_ref[j, pl.ds(r, blk_ref.shape[0], stride=d), :] = blk_ref[:, c0:c0 + LANES].astype(F32)
    return jnp.concatenate([tmp_ref[j] for j in range(n_tiles)], axis=1)


def _merge_kernel(x_ref, pa_ref, gb_ref, o1_ref, o2_ref, o3_ref, l1_ref, l2_ref, l3_ref,
                  wb_ref, wo_ref, g_ref, x1_ref, h2_ref, *tmp_refs):
    tmp = iter(tmp_refs)
    os_, ls_ = [], []
    for o_ref, l_ref, d in zip((o1_ref, o2_ref, o3_ref), (l1_ref, l2_ref, l3_ref), ATT_DILATIONS):
        os_.append(_from_residue_view(o_ref, next(tmp) if d > 1 else None, d))
        ls_.append(_from_residue_view(l_ref, next(tmp) if d > 1 else None, d))
    l1, l2, l3 = ls_
    m = jnp.maximum(jnp.maximum(l1, l2), l3)
    w1, w2, w3 = jnp.exp(l1 - m), jnp.exp(l2 - m), jnp.exp(l3 - m)
    yb = (w1 * os_[0] + w2 * os_[1] + w3 * os_[2]) / (w1 + w2 + w3)
    pb = jnp.dot(yb.astype(BF16), wb_ref[...], preferred_element_type=F32) * gb_ref[...].astype(F32)
    merged = (pa_ref[...].astype(F32) + pb).astype(BF16)
    x1 = x_ref[...] + jnp.dot(merged, wo_ref[...], preferred_element_type=F32)
    _store_row_split(x1_ref, x1)
    _store_row_split(h2_ref, _rms(x1, g_ref[...]))


def _merge(x2d, pa, gb, os_, ls_, wb_bf16, wo_bf16, g, tm=512):
    n = x2d.shape[0]
    wide = pl.BlockSpec((tm, D_MODEL), lambda i: (i, 0))
    views = [pl.BlockSpec((tm // d, d * ATT_KV_WIDTH), lambda i: (i, 0)) for d in ATT_DILATIONS]
    n_tmp = 2 * sum(d > 1 for d in ATT_DILATIONS)
    return pl.pallas_call(
        _merge_kernel,
        grid=(n // tm,),
        in_specs=[wide, wide, wide, *views, *views,
                  _full((ATT_KV_WIDTH, D_MODEL)), _full((D_MODEL, D_MODEL)), _full((1, D_MODEL))],
        out_specs=[_row_split_spec(tm), _row_split_spec(tm)],
        out_shape=[jax.ShapeDtypeStruct((n * SUBLANES, LANES), F32)] * 2,
        scratch_shapes=[pltpu.VMEM((ATT_KV_WIDTH // LANES, tm, LANES), F32)] * n_tmp,
        compiler_params=_params("parallel"),
        name="merge",
    )(x2d, pa, gb, *os_, *ls_, wb_bf16, wo_bf16, g)


def _topk_rows(s, k, payload=None):
    rows = lax.broadcasted_iota(I32, s.shape, 0).astype(F32)
    vals, ids = [], []
    for _ in range(k):
        m = jnp.max(s, axis=0, keepdims=True)
        am = jnp.min(jnp.where(s == m, rows, np.float32(s.shape[0])), axis=0, keepdims=True)
        hit = rows == am
        vals.append(m)
        if payload is None:
            ids.append(am)
        else:
            ids.append(jnp.max(jnp.where(hit, payload, -1.0), axis=0, keepdims=True))
        s = jnp.where(hit, -jnp.inf, s)
    return jnp.concatenate(vals, axis=0), jnp.concatenate(ids, axis=0)


def _pair_candidates(s0, s1, i0, i1):
    vals, ids = [], []
    for a in range(SUBLANES):
        n_b = PEER_TOPK // (a + 1)
        rows = PEER_TOPK if n_b > SUBLANES else SUBLANES
        v = s0[a:a + 1] + s1[0:rows]
        if n_b < rows:
            v = jnp.where(lax.broadcasted_iota(I32, v.shape, 0) < n_b, v, -jnp.inf)
        vals.append(v)
        ids.append(i0[a:a + 1] * PEER_N_KEYS + i1[0:rows])
    vals.append(s0[SUBLANES:PEER_TOPK] + s1[0:1])
    ids.append(i0[SUBLANES:PEER_TOPK] * PEER_N_KEYS + i1[0:1])
    return jnp.concatenate(vals, axis=0), jnp.concatenate(ids, axis=0)


def _split_bf16(a):
    hi = a.astype(BF16)
    return hi, (a - hi.astype(F32)).astype(BF16)


def _peer_topk_kernel(h_ref, wq_ref, skh_ref, skl_ref, idx_ref, gate_ref):
    q = jnp.dot(_load_row_split(h_ref).astype(BF16), wq_ref[...], preferred_element_type=F32)
    nt = (((1,), (1,)), ((), ()))
    idx_rows, gate_rows = [], []
    for hd in range(PEER_HEADS):
        top_s, top_i = [], []
        for p in range(2):
            hp = hd * 2 + p
            q_hi, q_lo = _split_bf16(q[:, hp * PEER_HALF:(hp + 1) * PEER_HALF])
            kh, kl = skh_ref[hp], skl_ref[hp]
            st = (lax.dot_general(kh, q_hi, nt, preferred_element_type=F32)
                  + lax.dot_general(kh, q_lo, nt, preferred_element_type=F32)
                  + lax.dot_general(kl, q_hi, nt, preferred_element_type=F32))
            ts, ti = _topk_rows(st, PEER_TOPK)
            top_s.append(ts)
            top_i.append(ti)
        cand, cidx = _pair_candidates(top_s[0], top_s[1], top_i[0], top_i[1])
        best_s, expert = _topk_rows(cand, PEER_TOPK, payload=cidx)
        e = jnp.exp(best_s - best_s[0:1])
        gate_rows.append(e / jnp.sum(e, axis=0, keepdims=True))
        idx_rows.append(expert * np.float32(SUBLANES))
    idx_ref[...] = jnp.concatenate(idx_rows, axis=0).T.astype(I32)
    gate_ref[...] = jnp.concatenate(gate_rows, axis=0).T


def _peer_topk(h2_split, wq_bf16, sk_hi, sk_lo, tm=256):
    n = h2_split.shape[0] // SUBLANES
    qw = wq_bf16.shape[1]
    out = pl.BlockSpec((tm, PEER_PICKS), lambda i: (i, 0))
    return pl.pallas_call(
        _peer_topk_kernel,
        grid=(n // tm,),
        in_specs=[_row_split_spec(tm),
                  _full((D_MODEL, qw)),
                  _full((2 * PEER_HEADS, PEER_N_KEYS, PEER_HALF)),
                  _full((2 * PEER_HEADS, PEER_N_KEYS, PEER_HALF))],
        out_specs=[out, out],
        out_shape=[jax.ShapeDtypeStruct((n, PEER_PICKS), I32), jax.ShapeDtypeStruct((n, PEER_PICKS), F32)],
        compiler_params=_params("parallel"),
        name="peer_topk",
    )(h2_split, wq_bf16, sk_hi, sk_lo)


def _pack_table(tab):
    return tab.astype(BF16).reshape(tab.shape[0] * SUBLANES, LANES)


def _gather_tile(tab_ref, e8):
    return tab_ref[pl.ds(pl.multiple_of(e8, SUBLANES), SUBLANES), :].astype(F32)


def _lane_sums(rows):
    return jnp.dot(rows.astype(BF16), jnp.ones((LANES, LANES), BF16), preferred_element_type=F32)


def _pick_diag():
    return (lax.broadcasted_iota(I32, (PEER_PICKS, LANES), 0)
            == lax.broadcasted_iota(I32, (PEER_PICKS, LANES), 1))


STAGE_TOKENS = 8
N_STAGE_BUFS = 4
TOKENS_PER_GROUP = N_STAGE_BUFS * STAGE_TOKENS


def _stage_copy(idx_ref, stage, bufs, sems, b):
    rows = pl.ds(pl.multiple_of(stage * STAGE_TOKENS, STAGE_TOKENS), STAGE_TOKENS)
    return pltpu.make_async_copy(idx_ref.at[rows], bufs[b], sems.at[b])


def _for_group_tokens(g, n_stages, idx_ref, bufs, sems, token_fn):
    for b in range(N_STAGE_BUFS):
        stage = N_STAGE_BUFS * g + b
        _stage_copy(idx_ref, stage, bufs, sems, b).wait()
        for tt in range(STAGE_TOKENS):
            local = b * STAGE_TOKENS + tt
            token_fn(local, g * TOKENS_PER_GROUP + local, lambda k, b=b, tt=tt: bufs[b][tt, k])

        @pl.when(stage + N_STAGE_BUFS < n_stages)
        def _():
            _stage_copy(idx_ref, stage + N_STAGE_BUFS, bufs, sems, b).start()


def _start_first_stages(idx_ref, bufs, sems):
    for b in range(N_STAGE_BUFS):
        _stage_copy(idx_ref, b, bufs, sems, b).start()


def _peer_u_kernel(idx_ref, h_ref, gate_ref, tab_ref, act_ref, slot_ref, *stage_refs):
    tm = gate_ref.shape[0]
    slot_rows = PEER_PICKS * ROW_SUBLANES
    diag = _pick_diag()
    bufs, sems = stage_refs[:-1], stage_refs[-1]
    _start_first_stages(idx_ref, bufs, sems)

    def token(local, t, pick_offset):
        hv = h_ref[pl.ds(pl.multiple_of(t * SUBLANES, SUBLANES), SUBLANES), :]
        for k in range(PEER_PICKS):
            y = _gather_tile(tab_ref, pick_offset(k)) * hv
            r0 = local * slot_rows + k * ROW_SUBLANES
            slot_ref[r0:r0 + ROW_SUBLANES, :] = y[0:ROW_SUBLANES] + y[ROW_SUBLANES:SUBLANES]

    def group(g, carry):
        _for_group_tokens(g, tm // STAGE_TOKENS, idx_ref, bufs, sems, token)
        rows = []
        for tt in range(TOKENS_PER_GROUP):
            r = slot_ref[pl.ds(tt * slot_rows, PEER_PICKS, stride=ROW_SUBLANES), :]
            for s in range(1, ROW_SUBLANES):
                r = r + slot_ref[pl.ds(tt * slot_rows + s, PEER_PICKS, stride=ROW_SUBLANES), :]
            rows.append(jnp.sum(jnp.where(diag, _lane_sums(r), 0.0), axis=0, keepdims=True))
        off = pl.multiple_of(g * TOKENS_PER_GROUP, TOKENS_PER_GROUP)
        a = jnp.concatenate(rows, axis=0)
        act_ref[pl.ds(off, TOKENS_PER_GROUP), :] = _gelu(a) * gate_ref[pl.ds(off, TOKENS_PER_GROUP), :]
        return carry

    lax.fori_loop(0, tm // TOKENS_PER_GROUP, group, 0)


def _stage_scratch():
    stage = pltpu.SMEM((STAGE_TOKENS, PEER_PICKS), I32)
    return [stage] * N_STAGE_BUFS + [pltpu.SemaphoreType.DMA((N_STAGE_BUFS,))]


def _peer_u(idx, h2_split, gate, tab, tm=512):
    n = idx.shape[0]
    tm = min(tm, n)
    picks = pl.BlockSpec((tm, PEER_PICKS), lambda i: (i, 0))
    return pl.pallas_call(
        _peer_u_kernel,
        grid=(n // tm,),
        in_specs=[picks, _row_split_spec(tm), picks, pl.BlockSpec(memory_space=pltpu.VMEM)],
        out_specs=picks,
        out_shape=jax.ShapeDtypeStruct((n, PEER_PICKS), F32),
        scratch_shapes=[pltpu.VMEM((TOKENS_PER_GROUP * PEER_PICKS * ROW_SUBLANES, LANES), F32), *_stage_scratch()],
        compiler_params=_params("arbitrary"),
        name="peer_u",
    )(idx, h2_split, gate, tab)


N_ACC = 4


def _peer_v_kernel(idx_ref, act_ref, x_ref, g_ref, tab_ref, out_ref, actb_ref, x2_ref, *stage_refs, normalize):
    tm = idx_ref.shape[0]
    diag = _pick_diag()
    bufs, sems = stage_refs[:-1], stage_refs[-1]
    _start_first_stages(idx_ref, bufs, sems)

    def token(local, t, pick_offset):
        acc = [jnp.zeros((SUBLANES, LANES), F32) for _ in range(N_ACC)]
        for k in range(PEER_PICKS):
            a = actb_ref[local * PEER_PICKS + k:local * PEER_PICKS + k + 1, :]
            acc[k % N_ACC] = acc[k % N_ACC] + a * _gather_tile(tab_ref, pick_offset(k))
        r = pl.multiple_of(t * SUBLANES, SUBLANES)
        x2_ref[pl.ds(r, SUBLANES), :] = x_ref[pl.ds(r, SUBLANES), :] + ((acc[0] + acc[1]) + (acc[2] + acc[3]))

    def group(g, carry):
        off = pl.multiple_of(g * TOKENS_PER_GROUP, TOKENS_PER_GROUP)
        for tt in range(TOKENS_PER_GROUP):
            spread = jnp.where(diag, act_ref[pl.ds(off + tt, 1), :], 0.0)
            actb_ref[tt * PEER_PICKS:(tt + 1) * PEER_PICKS, :] = _lane_sums(spread)
        _for_group_tokens(g, tm // STAGE_TOKENS, idx_ref, bufs, sems, token)
        return carry

    lax.fori_loop(0, tm // TOKENS_PER_GROUP, group, 0)
    x2 = _load_row_split(x2_ref)
    out_ref[...] = _rms(x2, g_ref[...]) if normalize else x2


def _peer_v(idx, act, x1_split, tab, final_g, tm=512):
    n = idx.shape[0]
    tm = min(tm, n)
    normalize = final_g is not None
    g = final_g if normalize else jnp.ones((1, D_MODEL), F32)
    picks = pl.BlockSpec((tm, PEER_PICKS), lambda i: (i, 0))
    return pl.pallas_call(
        functools.partial(_peer_v_kernel, normalize=normalize),
        grid=(n // tm,),
        in_specs=[picks, picks, _row_split_spec(tm), _full((1, D_MODEL)), pl.BlockSpec(memory_space=pltpu.VMEM)],
        out_specs=pl.BlockSpec((tm, D_MODEL), lambda i: (i, 0)),
        out_shape=jax.ShapeDtypeStruct((n, D_MODEL), F32),
        scratch_shapes=[pltpu.VMEM((TOKENS_PER_GROUP * PEER_PICKS, LANES), F32),
                        pltpu.VMEM((tm * SUBLANES, LANES), F32), *_stage_scratch()],
        compiler_params=_params("arbitrary"),
        name="peer_v",
    )(idx, act, x1_split, g, tab)


def kernel(x, norm_mix_g, w_in, sgu_norm_g, sgu_w, sgu_b, w_branch_a, w_branch_b, w_out,
           norm_ffn_g, peer_wq, peer_subkeys, peer_u, peer_v, norm_final_g):
    b, s, d = x.shape
    n = b * s
    x2 = x.reshape(n, d)
    for l in range(w_in.shape[0]):
        u, v, ga, gb, *qkv_views = _in_proj(x2, norm_mix_g[l].reshape(1, d), w_in[l].astype(BF16))
        bs_b = jnp.broadcast_to(sgu_b[l][:, :, None], (SGU_GROUPS, SGU_CHUNK, LANES))
        pa = _sgu(u, v, ga, sgu_norm_g[l].reshape(1, SGU_WIDTH), sgu_w[l].astype(BF16), bs_b,
                  w_branch_a[l].astype(BF16))
        os_, ls_ = [], []
        for g, (window, dilation) in enumerate(ATT_PATTERNS):
            qv, kv, vv = qkv_views[3 * g:3 * g + 3]
            o, lse = _attention_pattern(qv, kv, vv, b, dilation, window // (2 * dilation))
            os_.append(o)
            ls_.append(lse)
        x1, h2 = _merge(x2, pa, gb, os_, ls_, w_branch_b[l].astype(BF16), w_out[l].astype(BF16),
                        norm_ffn_g[l].reshape(1, d))
        sk = peer_subkeys[l].reshape(2 * PEER_HEADS, PEER_N_KEYS, PEER_HALF)
        sk_hi = sk.astype(BF16)
        sk_lo = (sk - sk_hi.astype(F32)).astype(BF16)
        idx, gate = _peer_topk(h2, peer_wq[l].astype(BF16), sk_hi, sk_lo)
        act = _peer_u(idx, h2, gate, _pack_table(peer_u[l]))
        last = l == w_in.shape[0] - 1
        x2 = _peer_v(idx, act, x1, _pack_table(peer_v[l]), norm_final_g.reshape(1, d) if last else None)
    return x2.reshape(b, s, d)
```

```python
import functools

import numpy as np
import jax
import jax.numpy as jnp
from jax import lax
from jax.experimental import pallas as pl
from jax.experimental.pallas import tpu as pltpu

F32 = jnp.float32
BF16 = jnp.bfloat16
I32 = jnp.int32

D_MODEL = 1024
EPS = 1e-6
NEG_INF = -1e30
SGU_CHUNK = 128
SGU_GROUPS = 8
SGU_WIDTH = 1024
ATT_HEADS = 8
ATT_HEAD_DIM = 64
ATT_PATTERNS = ((128, 1), (512, 4), (2048, 16))
ATT_GROUPS = 3
ATT_KV_WIDTH = ATT_HEADS * ATT_HEAD_DIM
ATT_Q_WIDTH = ATT_GROUPS * ATT_KV_WIDTH
PEER_HEADS = 8
PEER_N_KEYS = 128
PEER_N_EXPERTS = PEER_N_KEYS * PEER_N_KEYS
PEER_HALF = 128
PEER_TOPK = 16
PEER_PICKS = PEER_HEADS * PEER_TOPK

LANES = 128
SUBLANES = 8
ROW_SUBLANES = SUBLANES // 2
VMEM_LIMIT = 56 * 1024 * 1024

IN_CHUNK = 512
ATT_DILATIONS = tuple(d for _, d in ATT_PATTERNS)
_N_PLAIN = 4
_IN_LAYOUT = (
    ("gelu", (0, 0), ()), ("gelu", (0, 1), ()), ("gelu", (1, 0), ()), ("gelu", (1, 1), ()),
    *((None, None, ((_N_PLAIN + 3 * g, d),)) for g, d in enumerate(ATT_DILATIONS)),
    (None, None, tuple((_N_PLAIN + 3 * g + 1, d) for g, d in enumerate(ATT_DILATIONS))),
    (None, None, tuple((_N_PLAIN + 3 * g + 2, d) for g, d in enumerate(ATT_DILATIONS))),
    ("sigmoid", (2, 0), ()), ("sigmoid", (2, 1), ()), ("sigmoid", (3, 0), ()), ("sigmoid", (3, 1), ()),
)


def _params(*sem):
    return pltpu.CompilerParams(dimension_semantics=sem, vmem_limit_bytes=VMEM_LIMIT)


def _gelu(y):
    return 0.5 * y * (1.0 + lax.erf(y * np.float32(1.0 / np.sqrt(2.0))))


def _rms(x, g):
    return x * lax.rsqrt(jnp.mean(x * x, axis=-1, keepdims=True) + EPS) * g


def _full(shape):
    return pl.BlockSpec(shape, lambda *_: (0,) * len(shape))


def _store_row_split(ref, x):
    tm = x.shape[0]
    for s in range(SUBLANES):
        ref[pl.ds(s, tm, stride=SUBLANES), :] = x[:, s * LANES:(s + 1) * LANES]


def _load_row_split(ref):
    tm = ref.shape[0] // SUBLANES
    return jnp.concatenate([ref[pl.ds(s, tm, stride=SUBLANES), :] for s in range(SUBLANES)], axis=1)


def _row_split_spec(tm):
    return pl.BlockSpec((tm * SUBLANES, LANES), lambda i: (i, 0))


def _in_proj_kernel(x_ref, g_ref, w_ref, *refs):
    out_refs, ybuf_ref = refs[:-1], refs[-1]
    h = _rms(x_ref[...], g_ref[...]).astype(BF16)
    for c, (epi, plain, views) in enumerate(_IN_LAYOUT):
        y = jnp.dot(h, w_ref[:, c * IN_CHUNK:(c + 1) * IN_CHUNK], preferred_element_type=F32)
        if epi == "gelu":
            y = _gelu(y)
        elif epi == "sigmoid":
            y = jax.nn.sigmoid(y)
        if plain is not None:
            slot, sub = plain
            out_refs[slot][:, sub * IN_CHUNK:(sub + 1) * IN_CHUNK] = y.astype(BF16)
        if any(d > 1 for _, d in views):
            for j in range(IN_CHUNK // LANES):
                ybuf_ref[j] = y[:, j * LANES:(j + 1) * LANES]
        for slot, d in views:
            ref = out_refs[slot]
            if d == 1:
                ref[...] = y.astype(BF16)
                continue
            for r in range(d):
                for j in range(IN_CHUNK // LANES):
                    c0 = r * IN_CHUNK + j * LANES
                    ref[:, c0:c0 + LANES] = ybuf_ref[j, pl.ds(r, ref.shape[0], stride=d), :].astype(BF16)


def _in_proj(x2d, g, w_bf16, tm=512):
    n = x2d.shape[0]
    width = w_bf16.shape[1]
    shapes = [(n, SGU_WIDTH), (n, SGU_WIDTH), (n, D_MODEL), (n, D_MODEL)]
    blocks = [(tm, SGU_WIDTH), (tm, SGU_WIDTH), (tm, D_MODEL), (tm, D_MODEL)]
    for d in ATT_DILATIONS:
        shapes += [(n // d, d * IN_CHUNK)] * 3
        blocks += [(tm // d, d * IN_CHUNK)] * 3
    return pl.pallas_call(
        _in_proj_kernel,
        grid=(n // tm,),
        in_specs=[pl.BlockSpec((tm, D_MODEL), lambda i: (i, 0)),
                  _full((1, D_MODEL)),
                  pl.BlockSpec((D_MODEL, width), lambda i: (0, 0), pipeline_mode=pl.Buffered(1))],
        out_specs=[pl.BlockSpec(blk, lambda i: (i, 0)) for blk in blocks],
        out_shape=[jax.ShapeDtypeStruct(shp, BF16) for shp in shapes],
        scratch_shapes=[pltpu.VMEM((IN_CHUNK // LANES, tm, LANES), F32)],
        compiler_params=_params("parallel"),
        name="in_proj",
    )(x2d, g, w_bf16)


def _sgu_kernel(u_ref, v_ref, ga_ref, ng_ref, ws_ref, bs_ref, wa_ref, out_ref):
    tm = u_ref.shape[0]
    vn = _rms(v_ref[...].astype(F32), ng_ref[...]).astype(BF16)
    chunks = []
    for c in range(tm // SGU_CHUNK):
        cols = []
        for g in range(SGU_GROUPS):
            blk = vn[c * SGU_CHUNK:(c + 1) * SGU_CHUNK, g * LANES:(g + 1) * LANES]
            cols.append(jnp.dot(ws_ref[g], blk, preferred_element_type=F32) + bs_ref[g])
        chunks.append(jnp.concatenate(cols, axis=1))
    mixed = jnp.concatenate(chunks, axis=0)
    ya = (u_ref[...].astype(F32) * mixed).astype(BF16)
    pa = jnp.dot(ya, wa_ref[...], preferred_element_type=F32) * ga_ref[...].astype(F32)
    out_ref[...] = pa.astype(BF16)


def _sgu(u, v, ga, ng, ws_bf16, bs_b, wa_bf16, tm=512):
    n = u.shape[0]
    row = pl.BlockSpec((tm, SGU_WIDTH), lambda i: (i, 0))
    return pl.pallas_call(
        _sgu_kernel,
        grid=(n // tm,),
        in_specs=[row, row, row, _full((1, SGU_WIDTH)),
                  _full((SGU_GROUPS, SGU_CHUNK, SGU_CHUNK)),
                  _full((SGU_GROUPS, SGU_CHUNK, LANES)),
                  _full((SGU_WIDTH, D_MODEL))],
        out_specs=pl.BlockSpec((tm, D_MODEL), lambda i: (i, 0)),
        out_shape=jax.ShapeDtypeStruct((n, D_MODEL), BF16),
        compiler_params=_params("parallel"),
        name="sgu",
    )(u, v, ga, ng, ws_bf16, bs_b, wa_bf16)


def _attn_kernel(q_ref, kp_ref, kc_ref, kn_ref, vp_ref, vc_ref, vn_ref, o_ref, lse_ref,
                 *, dilation, n_side, sub_len):
    tq = q_ref.shape[0]
    win = tq + 2 * n_side
    i = pl.program_id(2)
    k = jnp.concatenate([kp_ref[...], kc_ref[...], kn_ref[...]], axis=0)
    v = jnp.concatenate([vp_ref[...], vc_ref[...], vn_ref[...]], axis=0)
    row = lax.broadcasted_iota(I32, (tq, win), 0)
    col = lax.broadcasted_iota(I32, (tq, win), 1)
    rel = col - n_side - row
    pos = i * tq - n_side + col
    valid = (jnp.abs(rel) <= n_side) & (pos >= 0) & (pos < sub_len)
    negdist = -(jnp.abs(rel) * dilation).astype(F32)
    lane = lax.broadcasted_iota(I32, (tq, LANES), 1)
    low_half = lane < ATT_HEAD_DIM
    scale = np.float32(ATT_HEAD_DIM ** -0.5)
    for pair in range(ATT_HEADS // 2):
        sl = slice(pair * LANES, (pair + 1) * LANES)
        qp, kpair, vpair = q_ref[:, sl], k[:, sl], v[:, sl]
        outs, lses = [], []
        for j in range(2):
            slope = np.float32(2.0 ** (-8.0 * (pair * 2 + j + 1) / ATT_HEADS))
            qm = jnp.where(low_half if j == 0 else jnp.logical_not(low_half), qp, jnp.zeros_like(qp))
            s = lax.dot_general(qm, kpair, (((1,), (1,)), ((), ())), preferred_element_type=F32)
            s = jnp.where(valid, s * scale + slope * negdist, NEG_INF)
            m = jnp.max(s, axis=-1, keepdims=True)
            p = jnp.exp(s - m)
            l = jnp.sum(p, axis=-1, keepdims=True)
            o = jnp.dot(p.astype(BF16), vpair, preferred_element_type=F32)
            outs.append(o / l)
            lses.append(jnp.broadcast_to(m + jnp.log(l), (tq, LANES)))
        o_ref[:, sl] = jnp.where(low_half, outs[0], outs[1]).astype(o_ref.dtype)
        lse_ref[:, sl] = jnp.where(low_half, lses[0], lses[1])


ATT_QUERY_TILE = 128


def _attention_pattern(qv, kv, vv, batch, dilation, n_side):
    sub_len = qv.shape[0] // batch
    tq = min(ATT_QUERY_TILE, sub_len)
    kvw = ATT_KV_WIDTH
    b = batch
    qv, kv, vv = (t.reshape(b, sub_len, dilation * kvw) for t in (qv, kv, vv))
    per = tq // n_side
    last = sub_len // n_side - 1
    cur = pl.BlockSpec((None, tq, kvw), lambda bi, r, i: (bi, i, r))
    prev = pl.BlockSpec((None, n_side, kvw), lambda bi, r, i: (bi, jnp.maximum(i * per - 1, 0), r))
    nxt = pl.BlockSpec((None, n_side, kvw), lambda bi, r, i: (bi, jnp.minimum((i + 1) * per, last), r))
    o, lse = pl.pallas_call(
        functools.partial(_attn_kernel, dilation=dilation, n_side=n_side, sub_len=sub_len),
        grid=(b, dilation, sub_len // tq),
        in_specs=[cur, prev, cur, nxt, prev, cur, nxt],
        out_specs=[cur, cur],
        out_shape=[jax.ShapeDtypeStruct((b, sub_len, dilation * kvw), BF16),
                   jax.ShapeDtypeStruct((b, sub_len, dilation * kvw), F32)],
        compiler_params=_params("parallel", "parallel", "parallel"),
        name=f"attn_d{dilation}",
    )(qv, kv, kv, kv, vv, vv, vv)
    return o.reshape(b * sub_len, dilation * kvw), lse.reshape(b * sub_len, dilation * kvw)


def _from_residue_view(blk_ref, tmp_ref, d):
    if d == 1:
        return blk_ref[...].astype(F32)
    n_tiles = IN_CHUNK // LANES
    for r in range(d):
        for j in range(n_tiles):
            c0 = r * IN_CHUNK + j * LANES
            tmp_ref[j, pl.ds(r, blk_ref.shape[0], stride=d), :] = blk_ref[:, c0:c0 + LANES].astype(F32)
    return jnp.concatenate([tmp_ref[j] for j in range(n_tiles)], axis=1)


def _merge_kernel(x_ref, pa_ref, gb_ref, o1_ref, o2_ref, o3_ref, l1_ref, l2_ref, l3_ref,
                  wb_ref, wo_ref, g_ref, x1_ref, h2_ref, *tmp_refs):
    tmp = iter(tmp_refs)
    os_, ls_ = [], []
    for o_ref, l_ref, d in zip((o1_ref, o2_ref, o3_ref), (l1_ref, l2_ref, l3_ref), ATT_DILATIONS):
        os_.append(_from_residue_view(o_ref, next(tmp) if d > 1 else None, d))
        ls_.append(_from_residue_view(l_ref, next(tmp) if d > 1 else None, d))
    l1, l2, l3 = ls_
    m = jnp.maximum(jnp.maximum(l1, l2), l3)
    w1, w2, w3 = jnp.exp(l1 - m), jnp.exp(l2 - m), jnp.exp(l3 - m)
    yb = (w1 * os_[0] + w2 * os_[1] + w3 * os_[2]) / (w1 + w2 + w3)
    pb = jnp.dot(yb.astype(BF16), wb_ref[...], preferred_element_type=F32) * gb_ref[...].astype(F32)
    merged = (pa_ref[...].astype(F32) + pb).astype(BF16)
    x1 = x_ref[...] + jnp.dot(merged, wo_ref[...], preferred_element_type=F32)
    _store_row_split(x1_ref, x1)
    _store_row_split(h2_ref, _rms(x1, g_ref[...]))


def _merge(x2d, pa, gb, os_, ls_, wb_bf16, wo_bf16, g, tm=512):
    n = x2d.shape[0]
    wide = pl.BlockSpec((tm, D_MODEL), lambda i: (i, 0))
    views = [pl.BlockSpec((tm // d, d * ATT_KV_WIDTH), lambda i: (i, 0)) for d in ATT_DILATIONS]
    n_tmp = 2 * sum(d > 1 for d in ATT_DILATIONS)
    return pl.pallas_call(
        _merge_kernel,
        grid=(n // tm,),
        in_specs=[wide, wide, wide, *views, *views,
                  _full((ATT_KV_WIDTH, D_MODEL)), _full((D_MODEL, D_MODEL)), _full((1, D_MODEL))],
        out_specs=[_row_split_spec(tm), _row_split_spec(tm)],
        out_shape=[jax.ShapeDtypeStruct((n * SUBLANES, LANES), F32)] * 2,
        scratch_shapes=[pltpu.VMEM((ATT_KV_WIDTH // LANES, tm, LANES), F32)] * n_tmp,
        compiler_params=_params("parallel"),
        name="merge",
    )(x2d, pa, gb, *os_, *ls_, wb_bf16, wo_bf16, g)


def _topk_rows(s, k, payload=None):
    rows = lax.broadcasted_iota(I32, s.shape, 0).astype(F32)
    vals, ids = [], []
    for _ in range(k):
        m = jnp.max(s, axis=0, keepdims=True)
        am = jnp.min(jnp.where(s == m, rows, np.float32(s.shape[0])), axis=0, keepdims=True)
        hit = rows == am
        vals.append(m)
        if payload is None:
            ids.append(am)
        else:
            ids.append(jnp.max(jnp.where(hit, payload, -1.0), axis=0, keepdims=True))
        s = jnp.where(hit, -jnp.inf, s)
    return jnp.concatenate(vals, axis=0), jnp.concatenate(ids, axis=0)


def _pair_candidates(s0, s1, i0, i1):
    vals, ids = [], []
    for a in range(SUBLANES):
        n_b = PEER_TOPK // (a + 1)
        rows = PEER_TOPK if n_b > SUBLANES else SUBLANES
        v = s0[a:a + 1] + s1[0:rows]
        if n_b < rows:
            v = jnp.where(lax.broadcasted_iota(I32, v.shape, 0) < n_b, v, -jnp.inf)
        vals.append(v)
        ids.append(i0[a:a + 1] * PEER_N_KEYS + i1[0:rows])
    vals.append(s0[SUBLANES:PEER_TOPK] + s1[0:1])
    ids.append(i0[SUBLANES:PEER_TOPK] * PEER_N_KEYS + i1[0:1])
    return jnp.concatenate(vals, axis=0), jnp.concatenate(ids, axis=0)


def _split_bf16(a):
    hi = a.astype(BF16)
    return hi, (a - hi.astype(F32)).astype(BF16)


def _peer_topk_kernel(h_ref, wq_ref, skh_ref, skl_ref, idx_ref, gate_ref):
    q = jnp.dot(_load_row_split(h_ref).astype(BF16), wq_ref[...], preferred_element_type=F32)
    nt = (((1,), (1,)), ((), ()))
    idx_rows, gate_rows = [], []
    for hd in range(PEER_HEADS):
        top_s, top_i = [], []
        for p in range(2):
            hp = hd * 2 + p
            q_hi, q_lo = _split_bf16(q[:, hp * PEER_HALF:(hp + 1) * PEER_HALF])
            kh, kl = skh_ref[hp], skl_ref[hp]
            st = (lax.dot_general(kh, q_hi, nt, preferred_element_type=F32)
                  + lax.dot_general(kh, q_lo, nt, preferred_element_type=F32)
                  + lax.dot_general(kl, q_hi, nt, preferred_element_type=F32))
            ts, ti = _topk_rows(st, PEER_TOPK)
            top_s.append(ts)
            top_i.append(ti)
        cand, cidx = _pair_candidates(top_s[0], top_s[1], top_i[0], top_i[1])
        best_s, expert = _topk_rows(cand, PEER_TOPK, payload=cidx)
        e = jnp.exp(best_s - best_s[0:1])
        gate_rows.append(e / jnp.sum(e, axis=0, keepdims=True))
        idx_rows.append(expert)
    idx_ref[...] = jnp.concatenate(idx_rows, axis=0).T.astype(I32)
    gate_ref[...] = jnp.concatenate(gate_rows, axis=0).T


def _peer_topk(h2_split, wq_bf16, sk_hi, sk_lo, tm=256):
    n = h2_split.shape[0] // SUBLANES
    qw = wq_bf16.shape[1]
    out = pl.BlockSpec((tm, PEER_PICKS), lambda i: (i, 0))
    return pl.pallas_call(
        _peer_topk_kernel,
        grid=(n // tm,),
        in_specs=[_row_split_spec(tm),
                  _full((D_MODEL, qw)),
                  _full((2 * PEER_HEADS, PEER_N_KEYS, PEER_HALF)),
                  _full((2 * PEER_HEADS, PEER_N_KEYS, PEER_HALF))],
        out_specs=[out, out],
        out_shape=[jax.ShapeDtypeStruct((n, PEER_PICKS), I32), jax.ShapeDtypeStruct((n, PEER_PICKS), F32)],
        compiler_params=_params("parallel"),
        name="peer_topk",
    )(h2_split, wq_bf16, sk_hi, sk_lo)


def _pack_table(tab):
    return tab.astype(BF16).reshape(tab.shape[0], SUBLANES, LANES)


def _gather_tile(tab_ref, e):
    return tab_ref[e].astype(F32)


def _lane_sums(rows):
    return jnp.dot(rows.astype(BF16), jnp.ones((LANES, LANES), BF16), preferred_element_type=F32)


def _pick_diag():
    return (lax.broadcasted_iota(I32, (PEER_PICKS, LANES), 0)
            == lax.broadcasted_iota(I32, (PEER_PICKS, LANES), 1))


STAGE_TOKENS = 8
N_STAGE_BUFS = 4
TOKENS_PER_GROUP = N_STAGE_BUFS * STAGE_TOKENS


def _stage_copy(idx_ref, stage, bufs, sems, b):
    rows = pl.ds(pl.multiple_of(stage * STAGE_TOKENS, STAGE_TOKENS), STAGE_TOKENS)
    return pltpu.make_async_copy(idx_ref.at[rows], bufs[b], sems.at[b])


def _for_group_tokens(g, n_stages, idx_ref, bufs, sems, token_fn):
    for b in range(N_STAGE_BUFS):
        stage = N_STAGE_BUFS * g + b
        _stage_copy(idx_ref, stage, bufs, sems, b).wait()
        for tt in range(STAGE_TOKENS):
            local = b * STAGE_TOKENS + tt
            token_fn(local, g * TOKENS_PER_GROUP + local, lambda k, b=b, tt=tt: bufs[b][tt, k])

        @pl.when(stage + N_STAGE_BUFS < n_stages)
        def _():
            _stage_copy(idx_ref, stage + N_STAGE_BUFS, bufs, sems, b).start()


def _start_first_stages(idx_ref, bufs, sems):
    for b in range(N_STAGE_BUFS):
        _stage_copy(idx_ref, b, bufs, sems, b).start()


def _peer_u_kernel(idx_ref, h_ref, gate_ref, tab_ref, act_ref, slot_ref, *stage_refs):
    tm = gate_ref.shape[0]
    slot_rows = PEER_PICKS * ROW_SUBLANES
    diag = _pick_diag()
    bufs, sems = stage_refs[:-1], stage_refs[-1]
    _start_first_stages(idx_ref, bufs, sems)

    def token(local, t, pick_offset):
        hv = h_ref[pl.ds(pl.multiple_of(t * SUBLANES, SUBLANES), SUBLANES), :]
        for k in range(PEER_PICKS):
            y = _gather_tile(tab_ref, pick_offset(k)) * hv
            r0 = local * slot_rows + k * ROW_SUBLANES
            slot_ref[r0:r0 + ROW_SUBLANES, :] = y[0:ROW_SUBLANES] + y[ROW_SUBLANES:SUBLANES]

    def group(g, carry):
        _for_group_tokens(g, tm // STAGE_TOKENS, idx_ref, bufs, sems, token)
        rows = []
        for tt in range(TOKENS_PER_GROUP):
            r = slot_ref[pl.ds(tt * slot_rows, PEER_PICKS, stride=ROW_SUBLANES), :]
            for s in range(1, ROW_SUBLANES):
                r = r + slot_ref[pl.ds(tt * slot_rows + s, PEER_PICKS, stride=ROW_SUBLANES), :]
            rows.append(jnp.sum(jnp.where(diag, _lane_sums(r), 0.0), axis=0, keepdims=True))
        off = pl.multiple_of(g * TOKENS_PER_GROUP, TOKENS_PER_GROUP)
        a = jnp.concatenate(rows, axis=0)
        act_ref[pl.ds(off, TOKENS_PER_GROUP), :] = _gelu(a) * gate_ref[pl.ds(off, TOKENS_PER_GROUP), :]
        return carry

    lax.fori_loop(0, tm // TOKENS_PER_GROUP, group, 0)


def _stage_scratch():
    stage = pltpu.SMEM((STAGE_TOKENS, PEER_PICKS), I32)
    return [stage] * N_STAGE_BUFS + [pltpu.SemaphoreType.DMA((N_STAGE_BUFS,))]


def _peer_u(idx, h2_split, gate, tab, tm=512):
    n = idx.shape[0]
    tm = min(tm, n)
    picks = pl.BlockSpec((tm, PEER_PICKS), lambda i: (i, 0))
    return pl.pallas_call(
        _peer_u_kernel,
        grid=(n // tm,),
        in_specs=[picks, _row_split_spec(tm), picks, pl.BlockSpec(memory_space=pltpu.VMEM)],
        out_specs=picks,
        out_shape=jax.ShapeDtypeStruct((n, PEER_PICKS), F32),
        scratch_shapes=[pltpu.VMEM((TOKENS_PER_GROUP * PEER_PICKS * ROW_SUBLANES, LANES), F32), *_stage_scratch()],
        compiler_params=_params("arbitrary"),
        name="peer_u",
    )(idx, h2_split, gate, tab)


N_ACC = 4


def _peer_v_kernel(idx_ref, act_ref, x_ref, g_ref, tab_ref, out_ref, actb_ref, x2_ref, *stage_refs, normalize):
    tm = idx_ref.shape[0]
    diag = _pick_diag()
    bufs, sems = stage_refs[:-1], stage_refs[-1]
    _start_first_stages(idx_ref, bufs, sems)

    def token(local, t, pick_offset):
        acc = [jnp.zeros((SUBLANES, LANES), F32) for _ in range(N_ACC)]
        for k in range(PEER_PICKS):
            a = actb_ref[local * PEER_PICKS + k:local * PEER_PICKS + k + 1, :]
            acc[k % N_ACC] = acc[k % N_ACC] + a * _gather_tile(tab_ref, pick_offset(k))
        r = pl.multiple_of(t * SUBLANES, SUBLANES)
        x2_ref[pl.ds(r, SUBLANES), :] = x_ref[pl.ds(r, SUBLANES), :] + ((acc[0] + acc[1]) + (acc[2] + acc[3]))

    def group(g, carry):
        off = pl.multiple_of(g * TOKENS_PER_GROUP, TOKENS_PER_GROUP)
        for tt in range(TOKENS_PER_GROUP):
            spread = jnp.where(diag, act_ref[pl.ds(off + tt, 1), :], 0.0)
            actb_ref[tt * PEER_PICKS:(tt + 1) * PEER_PICKS, :] = _lane_sums(spread)
        _for_group_tokens(g, tm // STAGE_TOKENS, idx_ref, bufs, sems, token)
        return carry

    lax.fori_loop(0, tm // TOKENS_PER_GROUP, group, 0)
    x2 = _load_row_split(x2_ref)
    out_ref[...] = _rms(x2, g_ref[...]) if normalize else x2


def _peer_v(idx, act, x1_split, tab, final_g, tm=512):
    n = idx.shape[0]
    tm = min(tm, n)
    normalize = final_g is not None
    g = final_g if normalize else jnp.ones((1, D_MODEL), F32)
    picks = pl.BlockSpec((tm, PEER_PICKS), lambda i: (i, 0))
    return pl.pallas_call(
        functools.partial(_peer_v_kernel, normalize=normalize),
        grid=(n // tm,),
        in_specs=[picks, picks, _row_split_spec(tm), _full((1, D_MODEL)), pl.BlockSpec(memory_space=pltpu.VMEM)],
        out_specs=pl.BlockSpec((tm, D_MODEL), lambda i: (i, 0)),
        out_shape=jax.ShapeDtypeStruct((n, D_MODEL), F32),
        scratch_shapes=[pltpu.VMEM((TOKENS_PER_GROUP * PEER_PICKS, LANES), F32),
                        pltpu.VMEM((tm * SUBLANES, LANES), F32), *_stage_scratch()],
        compiler_params=_params("arbitrary"),
        name="peer_v",
    )(idx, act, x1_split, g, tab)


def kernel(x, norm_mix_g, w_in, sgu_norm_g, sgu_w, sgu_b, w_branch_a, w_branch_b, w_out,
           norm_ffn_g, peer_wq, peer_subkeys, peer_u, peer_v, norm_final_g):
    b, s, d = x.shape
    n = b * s
    x2 = x.reshape(n, d)
    for l in range(w_in.shape[0]):
        u, v, ga, gb, *qkv_views = _in_proj(x2, norm_mix_g[l].reshape(1, d), w_in[l].astype(BF16))
        bs_b = jnp.broadcast_to(sgu_b[l][:, :, None], (SGU_GROUPS, SGU_CHUNK, LANES))
        pa = _sgu(u, v, ga, sgu_norm_g[l].reshape(1, SGU_WIDTH), sgu_w[l].astype(BF16), bs_b,
                  w_branch_a[l].astype(BF16))
        os_, ls_ = [], []
        for g, (window, dilation) in enumerate(ATT_PATTERNS):
            qv, kv, vv = qkv_views[3 * g:3 * g + 3]
            o, lse = _attention_pattern(qv, kv, vv, b, dilation, window // (2 * dilation))
            os_.append(o)
            ls_.append(lse)
        x1, h2 = _merge(x2, pa, gb, os_, ls_, w_branch_b[l].astype(BF16), w_out[l].astype(BF16),
                        norm_ffn_g[l].reshape(1, d))
        sk = peer_subkeys[l].reshape(2 * PEER_HEADS, PEER_N_KEYS, PEER_HALF)
        sk_hi = sk.astype(BF16)
        sk_lo = (sk - sk_hi.astype(F32)).astype(BF16)
        idx, gate = _peer_topk(h2, peer_wq[l].astype(BF16), sk_hi, sk_lo)
        act = _peer_u(idx, h2, gate, _pack_table(peer_u[l]))
        last = l == w_in.shape[0] - 1
        x2 = _peer_v(idx, act, x1, _pack_table(peer_v[l]), norm_final_g.reshape(1, d) if last else None)
    return x2.reshape(b, s, d)
```
